```python
import math
import jax
import jax.numpy as jnp
from jax import lax
import numpy as np

D_MODEL = 2048
BATCH = 1
SEQ = 16384
DEPTH = 4

CHUNK = 64
N_MIXERS = 2
N_A = (DEPTH + N_MIXERS - 1) // N_MIXERS
N_B = DEPTH // N_MIXERS

A_HEADS = 16
A_HEAD_DIM = D_MODEL // A_HEADS
A_LEFT_CHUNKS = 8
A_PAD = A_LEFT_CHUNKS * CHUNK
A_BAND = (A_LEFT_CHUNKS + 1) * CHUNK
A_REL_CLIP = 128

B_HEADS = 8
B_QK_DIM = D_MODEL // (2 * B_HEADS)
B_V_DIM = 2 * B_QK_DIM
Q_BLOCK = 128

T5_BUCKETS = 32
T5_MAX_DIST = 128

D_FF = 4 * D_MODEL

EPS = 1e-6
NEG_INF = -1e30

kernel_name = "hybrid_chunkband_diffattn_sqrelu_trunk"


def rmsnorm(x, g):
    xf = x.astype(jnp.float32)
    y = xf * lax.rsqrt(jnp.mean(xf * xf, axis=-1, keepdims=True) + EPS)
    return (y * g.astype(jnp.float32)).astype(x.dtype)


def lambda_init_fn(layer):
    return 0.8 - 0.6 * math.exp(-0.3 * layer)


def t5_bucket(rel):
    half = T5_BUCKETS // 2
    max_exact = half // 2
    ret = (rel > 0).astype(jnp.int32) * half
    n = jnp.abs(rel)
    nf = jnp.maximum(n, 1).astype(jnp.float32)
    large = max_exact + (jnp.log(nf / max_exact) / math.log(T5_MAX_DIST / max_exact)
                         * (half - max_exact)).astype(jnp.int32)
    large = jnp.minimum(large, half - 1)
    return ret + jnp.where(n < max_exact, n, large)


def chunk_band_attention(h, w_qkv, w_o, rel_bias):
    b, s, _ = h.shape
    q, k, v = jnp.split(h @ w_qkv, 3, axis=-1)
    q = q.reshape(b, s, A_HEADS, A_HEAD_DIM)
    k = jnp.pad(k.reshape(b, s, A_HEADS, A_HEAD_DIM), ((0, 0), (A_PAD, 0), (0, 0), (0, 0)))
    v = jnp.pad(v.reshape(b, s, A_HEADS, A_HEAD_DIM), ((0, 0), (A_PAD, 0), (0, 0), (0, 0)))
    rel = (jnp.arange(A_BAND) - A_PAD)[None, :] - jnp.arange(CHUNK)[:, None]
    idx = jnp.clip(rel, -A_REL_CLIP, A_REL_CLIP) + A_REL_CLIP
    bias = jnp.transpose(rel_bias[idx], (2, 0, 1)).astype(jnp.float32)
    scale = A_HEAD_DIM ** -0.5
    band_offs = jnp.arange(A_BAND) - A_PAD

    def one_chunk(c):
        start = c * CHUNK
        qc = lax.dynamic_slice_in_dim(q, start, CHUNK, axis=1)
        kc = lax.dynamic_slice_in_dim(k, start, A_BAND, axis=1)
        vc = lax.dynamic_slice_in_dim(v, start, A_BAND, axis=1)
        logits = jnp.einsum('bqhd,bkhd->bhqk', qc, kc).astype(jnp.float32) * scale + bias
        valid = (start + band_offs) >= 0
        logits = jnp.where(valid[None, None, None, :], logits, NEG_INF)
        p = jax.nn.softmax(logits, axis=-1).astype(vc.dtype)
        return jnp.einsum('bhqk,bkhd->bqhd', p, vc)

    o = lax.map(one_chunk, jnp.arange(s // CHUNK))
    o = jnp.moveaxis(o, 0, 1).reshape(b, s, D_MODEL)
    return o @ w_o


def diff_attention(h, w_qkv, w_o, lam, subln_g, t5_table, lambda_init):
    b, s, _ = h.shape
    q, k, v = jnp.split(h @ w_qkv, 3, axis=-1)
    q = q.reshape(b, s, B_HEADS, 2, B_QK_DIM)
    k = k.reshape(b, s, B_HEADS, 2, B_QK_DIM)
    v = v.reshape(b, s, B_HEADS, B_V_DIM)
    lf = lam.astype(jnp.float32)
    lam_full = jnp.exp(jnp.sum(lf[0] * lf[1])) - jnp.exp(jnp.sum(lf[2] * lf[3])) + lambda_init
    scale = B_QK_DIM ** -0.5
    k_pos = jnp.arange(s)
    k_chunk = k_pos // CHUNK

    def one_block(blk):
        start = blk * Q_BLOCK
        qb = lax.dynamic_slice_in_dim(q, start, Q_BLOCK, axis=1)
        logits = jnp.einsum('bqhmd,bkhmd->bmhqk', qb, k).astype(jnp.float32) * scale
        q_pos = start + jnp.arange(Q_BLOCK)
        rel = k_pos[None, :] - q_pos[:, None]
        bias = jnp.transpose(t5_table[t5_bucket(rel)], (2, 0, 1)).astype(jnp.float32)
        allowed = k_chunk[None, :] <= (q_pos // CHUNK)[:, None]
        logits = jnp.where(allowed, logits + bias, NEG_INF)
        p = jax.nn.softmax(logits, axis=-1)
        attn = p[:, 0] - lam_full * p[:, 1]
        return jnp.einsum('bhqk,bkhe->bqhe', attn.astype(v.dtype), v)

    o = lax.map(one_block, jnp.arange(s // Q_BLOCK))
    o = jnp.moveaxis(o, 0, 1).reshape(b, s, B_HEADS, B_V_DIM)
    o = rmsnorm(o, subln_g) * (1.0 - lambda_init)
    return o.reshape(b, s, D_MODEL) @ w_o


def squared_relu_mlp(h, w_up, w_down):
    return jnp.square(jax.nn.relu(h @ w_up)) @ w_down


def setup_inputs(seed: int = 0) -> dict:
    key = jax.random.key(seed)
    ks = jax.random.split(key, 12)
    f32 = jnp.float32
    x = jax.random.normal(ks[0], (BATCH, SEQ, D_MODEL), f32)
    norm_g = 1.0 + 0.05 * jax.random.normal(ks[1], (DEPTH, 4, D_MODEL), f32)
    a_w_qkv = jax.random.normal(ks[2], (N_A, D_MODEL, 3 * D_MODEL), f32) * D_MODEL ** -0.5
    a_w_o = jax.random.normal(ks[3], (N_A, D_MODEL, D_MODEL), f32) * D_MODEL ** -0.5
    a_rel_bias = 0.5 * jax.random.normal(ks[4], (N_A, 2 * A_REL_CLIP + 1, A_HEADS), f32)
    b_w_qkv = jax.random.normal(ks[5], (N_B, D_MODEL, 3 * D_MODEL), f32) * D_MODEL ** -0.5
    b_w_o = jax.random.normal(ks[6], (N_B, D_MODEL, D_MODEL), f32) * D_MODEL ** -0.5
    b_lambda = 0.1 * jax.random.normal(ks[7], (N_B, 4, B_QK_DIM), f32)
    b_subln_g = 1.0 + 0.05 * jax.random.normal(ks[8], (N_B, B_V_DIM), f32)
    t5_bias = 0.5 * jax.random.normal(ks[9], (T5_BUCKETS, B_HEADS), f32)
    w_up = jax.random.normal(ks[10], (DEPTH, D_MODEL, D_FF), f32) * D_MODEL ** -0.5
    w_down = jax.random.normal(ks[11], (DEPTH, D_FF, D_MODEL), f32) * D_FF ** -0.5
    return {"x": x, "norm_g": norm_g, "a_w_qkv": a_w_qkv, "a_w_o": a_w_o,
            "a_rel_bias": a_rel_bias, "b_w_qkv": b_w_qkv, "b_w_o": b_w_o,
            "b_lambda": b_lambda, "b_subln_g": b_subln_g, "t5_bias": t5_bias,
            "w_up": w_up, "w_down": w_down}


def reference(x, norm_g, a_w_qkv, a_w_o, a_rel_bias, b_w_qkv, b_w_o, b_lambda,
              b_subln_g, t5_bias, w_up, w_down):
    h = x
    for layer in range(DEPTH):
        g = norm_g[layer]
        i = layer // N_MIXERS
        y = rmsnorm(h, g[0])
        if layer % N_MIXERS == 0:
            y = chunk_band_attention(y, a_w_qkv[i], a_w_o[i], a_rel_bias[i])
        else:
            y = diff_attention(y, b_w_qkv[i], b_w_o[i], b_lambda[i], b_subln_g[i],
                               t5_bias, lambda_init_fn(layer))
        h = h + rmsnorm(y, g[1])
        y = rmsnorm(h, g[2])
        h = h + rmsnorm(squared_relu_mlp(y, w_up[layer], w_down[layer]), g[3])
    return h
```

```python
import functools
import math

import jax
import jax.numpy as jnp
from jax import lax
from jax.experimental import pallas as pl
from jax.experimental.pallas import tpu as pltpu

EPS = 1e-6
NEG_INF = -1e30

CHUNK = 64
A_HEADS = 16
A_LEFT_CHUNKS = 8
A_REL_CLIP = 128
B_HEADS = 8
T5_BUCKETS = 32
T5_MAX_DIST = 128

QKV_TM, QKV_TN = 1024, 1024
OPROJ_TM = 512
MLP_TM, MLP_TF = 512, 512
A_TQ = 256
A_WIN_BLOCKS = 3
A_HEAD_GROUP = 8
B_T = 512

MIB = 1024 * 1024


def _dot(a, b):
    return jnp.dot(a, b, preferred_element_type=jnp.float32)


def _dot_nt(a, b):
    return lax.dot_general(a, b, (((1,), (1,)), ((), ())),
                           preferred_element_type=jnp.float32)


def _rmsnorm_f32(x, g):
    return x * lax.rsqrt(jnp.mean(x * x, axis=-1, keepdims=True) + EPS) * g


def _norm_matmul_kernel(x_ref, g_ref, w_ref, o_ref, xn_ref):
    @pl.when(pl.program_id(1) == 0)
    def _():
        xn_ref[...] = _rmsnorm_f32(x_ref[...], g_ref[...]).astype(xn_ref.dtype)

    o_ref[...] = _dot(xn_ref[...], w_ref[...]).astype(o_ref.dtype)


def _norm_matmul(x, g, w):
    s, d = x.shape
    n = w.shape[1]
    tm, tn = min(QKV_TM, s), min(QKV_TN, n)
    return pl.pallas_call(
        _norm_matmul_kernel,
        out_shape=jax.ShapeDtypeStruct((s, n), jnp.bfloat16),
        grid=(s // tm, n // tn),
        in_specs=[
            pl.BlockSpec((tm, d), lambda i, j: (i, 0)),
            pl.BlockSpec((1, d), lambda i, j: (0, 0)),
            pl.BlockSpec((d, tn), lambda i, j: (0, j)),
        ],
        out_specs=pl.BlockSpec((tm, tn), lambda i, j: (i, j)),
        scratch_shapes=[pltpu.VMEM((tm, d), jnp.bfloat16)],
        compiler_params=pltpu.CompilerParams(
            dimension_semantics=("parallel", "arbitrary"),
            vmem_limit_bytes=48 * MIB),
        name="norm_qkv",
    )(x, g, w)


def _band_attn_kernel(q_ref, k0_ref, k1_ref, k2_ref, v0_ref, v1_ref, v2_ref,
                      bias_ref, o_ref, *, heads, dh, tq):
    i = pl.program_id(1)
    k_refs = (k0_ref, k1_ref, k2_ref)
    v_refs = (v0_ref, v1_ref, v2_ref)
    nb = len(k_refs)
    pad = [jnp.where(b < (nb - 1) - i, NEG_INF, 0.0).astype(jnp.float32)
           for b in range(nb)]
    for h in range(heads):
        sl = slice(h * dh, (h + 1) * dh)
        q = q_ref[:, sl]
        s = []
        for b in range(nb):
            sb = _dot_nt(q, k_refs[b][:, sl]) + bias_ref[h, :, b * tq:(b + 1) * tq]
            if b < nb - 1:
                sb = sb + pad[b]
            s.append(sb)
        m = s[0].max(axis=-1, keepdims=True)
        for b in range(1, nb):
            m = jnp.maximum(m, s[b].max(axis=-1, keepdims=True))
        l = None
        o = None
        for b in range(nb):
            p = jnp.exp(s[b] - m)
            pl_sum = p.sum(axis=-1, keepdims=True)
            po = _dot(p.astype(jnp.bfloat16), v_refs[b][:, sl])
            l = pl_sum if l is None else l + pl_sum
            o = po if o is None else o + po
        o_ref[:, sl] = (o / l).astype(o_ref.dtype)


def _band_attn(qkv, bias, d_model):
    s = qkv.shape[0]
    tq = A_TQ
    hg = A_HEAD_GROUP
    dh = d_model // A_HEADS
    gw = hg * dh
    n_groups = A_HEADS // hg
    ncol = d_model // gw
    nb = A_WIN_BLOCKS

    def kv_spec(section, b):
        return pl.BlockSpec(
            (tq, gw),
            lambda g, i: (jnp.maximum(i - (nb - 1) + b, 0), section * ncol + g))

    kernel = functools.partial(_band_attn_kernel, heads=hg, dh=dh, tq=tq)
    return pl.pallas_call(
        kernel,
        out_shape=jax.ShapeDtypeStruct((s, d_model), jnp.bfloat16),
        grid=(n_groups, s // tq),
        in_specs=[pl.BlockSpec((tq, gw), lambda g, i: (i, g))]
        + [kv_spec(1, b) for b in range(nb)]
        + [kv_spec(2, b) for b in range(nb)]
        + [pl.BlockSpec((hg, tq, nb * tq), lambda g, i: (g, 0, 0))],
        out_specs=pl.BlockSpec((tq, gw), lambda g, i: (i, g)),
        compiler_params=pltpu.CompilerParams(
            dimension_semantics=("parallel", "arbitrary"),
            vmem_limit_bytes=48 * MIB),
        name="band_attn",
    )(qkv, qkv, qkv, qkv, qkv, qkv, qkv, bias)


def _band_bias_table(rel_bias):
    tq, win = A_TQ, A_WIN_BLOCKS * A_TQ
    left = (A_WIN_BLOCKS - 1) * A_TQ
    qq = jnp.arange(tq)[:, None]
    kk = jnp.arange(win)[None, :]
    rel = (kk - left) - qq
    idx = jnp.clip(rel, -A_REL_CLIP, A_REL_CLIP) + A_REL_CLIP
    qc = qq // CHUNK
    kc = (kk - left + A_LEFT_CHUNKS * CHUNK) // CHUNK
    allowed = (kc >= qc) & (kc <= qc + A_LEFT_CHUNKS)
    bias = jnp.transpose(rel_bias[idx], (2, 0, 1)).astype(jnp.float32)
    return jnp.where(allowed[None], bias, NEG_INF)


def _diff_attn_kernel(lam_ref, g_ref, q_ref, k_ref, v_ref, bd_ref, bl_ref, o_ref,
                      m_ref, l_ref, acc_ref, *, t, d, lambda_init):
    i = pl.program_id(1)
    m_ref[...] = jnp.full(m_ref.shape, NEG_INF, jnp.float32)
    l_ref[...] = jnp.zeros(l_ref.shape, jnp.float32)
    acc_ref[...] = jnp.zeros(acc_ref.shape, jnp.float32)

    def step(j, bias):
        row = pl.multiple_of(j * t, t)
        kt = k_ref[pl.ds(row, t), :]
        vt = v_ref[pl.ds(row, t), :]
        for mp in range(2):
            s = _dot_nt(q_ref[:, mp * d:(mp + 1) * d], kt[:, mp * d:(mp + 1) * d])
            if bias is not None:
                s = s + bias
            m_old = m_ref[mp]
            m_new = jnp.maximum(m_old, s.max(axis=-1, keepdims=True))
            alpha = jnp.exp(m_old - m_new)
            p = jnp.exp(s - m_new)
            l_ref[mp] = alpha * l_ref[mp] + p.sum(axis=-1, keepdims=True)
            acc_ref[mp] = alpha * acc_ref[mp] + _dot(p.astype(jnp.bfloat16), vt)
            m_ref[mp] = m_new

    def body(j, carry):
        step(j, None)
        return carry

    lax.fori_loop(0, jnp.maximum(i - 1, 0), body, 0)

    @pl.when(i >= 1)
    def _():
        step(i - 1, bl_ref[0])

    step(i, bd_ref[0])

    lam = lam_ref[...]
    lam_full = (jnp.exp(jnp.sum(lam[0:1] * lam[1:2])) - jnp.exp(jnp.sum(lam[2:3] * lam[3:4]))
                + lambda_init)
    o = acc_ref[0] / l_ref[0] - lam_full * (acc_ref[1] / l_ref[1])
    o = _rmsnorm_f32(o, g_ref[...]) * (1.0 - lambda_init)
    o_ref[...] = o.astype(o_ref.dtype)


def _diff_attn(qkv, lam, subln_g, bias_diag, bias_left, d_model, lambda_init):
    s = qkv.shape[0]
    t = B_T
    e = d_model // B_HEADS
    d = e // 2
    kernel = functools.partial(_diff_attn_kernel, t=t, d=d, lambda_init=lambda_init)
    return pl.pallas_call(
        kernel,
        out_shape=jax.ShapeDtypeStruct((s, d_model), jnp.bfloat16),
        grid=(B_HEADS, s // t),
        in_specs=[
            pl.BlockSpec((4, d), lambda h, i: (0, 0)),
            pl.BlockSpec((1, e), lambda h, i: (0, 0)),
            pl.BlockSpec((t, e), lambda h, i: (i, h)),
            pl.BlockSpec((s, e), lambda h, i: (0, B_HEADS + h)),
            pl.BlockSpec((s, e), lambda h, i: (0, 2 * B_HEADS + h)),
            pl.BlockSpec((1, t, t), lambda h, i: (h, 0, 0)),
            pl.BlockSpec((1, t, t), lambda h, i: (h, 0, 0)),
        ],
        out_specs=pl.BlockSpec((t, e), lambda h, i: (i, h)),
        scratch_shapes=[
            pltpu.VMEM((2, t, 1), jnp.float32),
            pltpu.VMEM((2, t, 1), jnp.float32),
            pltpu.VMEM((2, t, e), jnp.float32),
        ],
        compiler_params=pltpu.CompilerParams(
            dimension_semantics=("parallel", "arbitrary"),
            vmem_limit_bytes=56 * MIB),
        name="diff_attn",
    )(lam, subln_g, qkv, qkv, qkv, bias_diag, bias_left)


def _t5_bucket(rel):
    half = T5_BUCKETS // 2
    max_exact = half // 2
    ret = (rel > 0).astype(jnp.int32) * half
    n = jnp.abs(rel)
    nf = jnp.maximum(n, 1).astype(jnp.float32)
    large = max_exact + (jnp.log(nf / max_exact) / math.log(T5_MAX_DIST / max_exact)
                         * (half - max_exact)).astype(jnp.int32)
    large = jnp.minimum(large, half - 1)
    return ret + jnp.where(n < max_exact, n, large)


def _diff_bias_tables(t5_table):
    t = B_T
    qq = jnp.arange(t)[:, None]
    kk = jnp.arange(t)[None, :]
    far = t5_table[T5_BUCKETS // 2 - 1]

    def table(rel):
        b = t5_table[_t5_bucket(rel)] - far
        return jnp.transpose(b, (2, 0, 1)).astype(jnp.float32)

    diag = jnp.where((kk // CHUNK <= qq // CHUNK)[None], table(kk - qq), NEG_INF)
    left = table(kk - qq - t)
    return diag, left


def _oproj_kernel(a_ref, w_ref, h_ref, g_ref, o_ref):
    y = _dot(a_ref[...], w_ref[...])
    o_ref[...] = h_ref[...] + _rmsnorm_f32(y, g_ref[...])


def _oproj(a, w, h, g):
    s, d = h.shape
    tm = min(OPROJ_TM, s)
    return pl.pallas_call(
        _oproj_kernel,
        out_shape=jax.ShapeDtypeStruct((s, d), jnp.float32),
        grid=(s // tm,),
        in_specs=[
            pl.BlockSpec((tm, d), lambda i: (i, 0)),
            pl.BlockSpec((d, d), lambda i: (0, 0)),
            pl.BlockSpec((tm, d), lambda i: (i, 0)),
            pl.BlockSpec((1, d), lambda i: (0, 0)),
        ],
        out_specs=pl.BlockSpec((tm, d), lambda i: (i, 0)),
        compiler_params=pltpu.CompilerParams(
            dimension_semantics=("parallel",),
            vmem_limit_bytes=48 * MIB),
        name="oproj",
    )(a, w, h, g)


def _mlp_kernel(h_ref, g_in_ref, wu_ref, wd_ref, g_out_ref, o_ref, xn_ref, acc_ref):
    c = pl.program_id(1)

    @pl.when(c == 0)
    def _():
        xn_ref[...] = _rmsnorm_f32(h_ref[...], g_in_ref[...]).astype(xn_ref.dtype)
        acc_ref[...] = jnp.zeros(acc_ref.shape, jnp.float32)

    u = jnp.maximum(_dot(xn_ref[...], wu_ref[...]), 0.0)
    acc_ref[...] += _dot((u * u).astype(jnp.bfloat16), wd_ref[...])

    @pl.when(c == pl.num_programs(1) - 1)
    def _():
        o_ref[...] = h_ref[...] + _rmsnorm_f32(acc_ref[...], g_out_ref[...])


def _mlp(h, g_in, wu, wd, g_out):
    s, d = h.shape
    f = wu.shape[1]
    tm, tf = min(MLP_TM, s), min(MLP_TF, f)
    return pl.pallas_call(
        _mlp_kernel,
        out_shape=jax.ShapeDtypeStruct((s, d), jnp.float32),
        grid=(s // tm, f // tf),
        in_specs=[
            pl.BlockSpec((tm, d), lambda i, c: (i, 0)),
            pl.BlockSpec((1, d), lambda i, c: (0, 0)),
            pl.BlockSpec((d, tf), lambda i, c: (0, c)),
            pl.BlockSpec((tf, d), lambda i, c: (c, 0)),
            pl.BlockSpec((1, d), lambda i, c: (0, 0)),
        ],
        out_specs=pl.BlockSpec((tm, d), lambda i, c: (i, 0)),
        scratch_shapes=[
            pltpu.VMEM((tm, d), jnp.bfloat16),
            pltpu.VMEM((tm, d), jnp.float32),
        ],
        compiler_params=pltpu.CompilerParams(
            dimension_semantics=("parallel", "arbitrary"),
            vmem_limit_bytes=48 * MIB),
        name="mlp",
    )(h, g_in, wu, wd, g_out)


def _scaled_qkv_weight(w_qkv, d_model, head_dim):
    scale = head_dim ** -0.5
    col_scale = jnp.where(jnp.arange(w_qkv.shape[1]) < d_model, scale, 1.0)
    return (w_qkv * col_scale[None, :].astype(w_qkv.dtype)).astype(jnp.bfloat16)


def kernel(x, norm_g, a_w_qkv, a_w_o, a_rel_bias, b_w_qkv, b_w_o, b_lambda, b_subln_g,
           t5_bias, w_up, w_down):
    batch, seq, d_model = x.shape
    depth = norm_g.shape[0]
    n_mixers = 2
    assert seq % max(QKV_TM, OPROJ_TM, MLP_TM, A_TQ, B_T) == 0
    assert A_TQ * (A_WIN_BLOCKS - 1) >= A_LEFT_CHUNKS * CHUNK and A_TQ % CHUNK == 0
    assert B_T % CHUNK == 0 and B_T >= T5_MAX_DIST

    bias_diag, bias_left = _diff_bias_tables(t5_bias)
    outs = []
    for b in range(batch):
        h = x[b]
        for layer in range(depth):
            g = norm_g[layer]
            i = layer // n_mixers
            if layer % n_mixers == 0:
                wqkv = _scaled_qkv_weight(a_w_qkv[i], d_model, d_model // A_HEADS)
                qkv = _norm_matmul(h, g[0:1], wqkv)
                a = _band_attn(qkv, _band_bias_table(a_rel_bias[i]), d_model)
                wo = a_w_o[i]
            else:
                wqkv = _scaled_qkv_weight(b_w_qkv[i], d_model, d_model // (2 * B_HEADS))
                qkv = _norm_matmul(h, g[0:1], wqkv)
                lambda_init = 0.8 - 0.6 * math.exp(-0.3 * layer)
                a = _diff_attn(qkv, b_lambda[i], b_subln_g[i][None, :], bias_diag, bias_left,
                               d_model, lambda_init)
                wo = b_w_o[i]
            h = _oproj(a, wo.astype(jnp.bfloat16), h, g[1:2])
            h = _mlp(h, g[2:3], w_up[layer].astype(jnp.bfloat16),
                     w_down[layer].astype(jnp.bfloat16), g[3:4])
        outs.append(h)
    return jnp.stack(outs, axis=0)
```

```python
import functools
import math

import jax
import jax.numpy as jnp
from jax import lax
from jax.experimental import pallas as pl
from jax.experimental.pallas import tpu as pltpu

EPS = 1e-6
NEG_INF = -1e30
LOG2E = math.log2(math.e)
LANES = 128

CHUNK = 64
A_HEADS = 16
A_LEFT_CHUNKS = 8
A_REL_CLIP = 128
B_HEADS = 8
T5_BUCKETS = 32
T5_MAX_DIST = 128

QKV_TM, QKV_TN = 1024, 1024
OPROJ_TM = 512
MLP_TM, MLP_TF = 512, 512
A_TQ = 256
A_WIN_BLOCKS = 3
A_HEAD_GROUP = 8
B_T = 512
B_NSUB = 2

MIB = 1024 * 1024


def _dot(a, b):
    return jnp.dot(a, b, preferred_element_type=jnp.float32)


def _dot_nt(a, b):
    return lax.dot_general(a, b, (((1,), (1,)), ((), ())),
                           preferred_element_type=jnp.float32)


def _rmsnorm_f32(x, g):
    return x * lax.rsqrt(jnp.mean(x * x, axis=-1, keepdims=True) + EPS) * g


def _lane_tile(x, n):
    return x if n == 1 else jnp.concatenate([x] * n, axis=1)


def _lane_fold_sum(x):
    acc = x[:, :LANES]
    for c in range(1, x.shape[1] // LANES):
        acc = acc + x[:, c * LANES:(c + 1) * LANES]
    return acc


def _toeplitz(f, nrow, ncol):
    h, l = f.shape
    assert l == nrow + ncol - 1
    g = jnp.concatenate([f, jnp.zeros((h, 1), f.dtype)], axis=1)
    g = jnp.roll(g, -(nrow - 1), axis=1)
    flat = jnp.tile(g, (1, nrow))[:, :nrow * l]
    return flat.reshape(h, nrow, l)[:, :, :ncol]


def _norm_matmul_kernel(x_ref, g_ref, w_ref, o_ref, xn_ref):
    @pl.when(pl.program_id(1) == 0)
    def _():
        xn_ref[...] = _rmsnorm_f32(x_ref[...], g_ref[...]).astype(xn_ref.dtype)

    o_ref[...] = _dot(xn_ref[...], w_ref[...]).astype(o_ref.dtype)


def _norm_matmul(x, g, w):
    s, d = x.shape
    n = w.shape[1]
    tm, tn = min(QKV_TM, s), min(QKV_TN, n)
    return pl.pallas_call(
        _norm_matmul_kernel,
        out_shape=jax.ShapeDtypeStruct((s, n), jnp.bfloat16),
        grid=(s // tm, n // tn),
        in_specs=[
            pl.BlockSpec((tm, d), lambda i, j: (i, 0)),
            pl.BlockSpec((1, d), lambda i, j: (0, 0)),
            pl.BlockSpec((d, tn), lambda i, j: (0, j)),
        ],
        out_specs=pl.BlockSpec((tm, tn), lambda i, j: (i, j)),
        scratch_shapes=[pltpu.VMEM((tm, d), jnp.bfloat16)],
        compiler_params=pltpu.CompilerParams(
            dimension_semantics=("parallel", "arbitrary"),
            vmem_limit_bytes=48 * MIB),
        name="norm_qkv",
    )(x, g, w)


def _band_attn_kernel(q_ref, k0_ref, k1_ref, k2_ref, v0_ref, v1_ref, v2_ref,
                      bias_ref, o_ref, *, heads, dh, tq):
    i = pl.program_id(1)
    k_refs = (k0_ref, k1_ref, k2_ref)
    v_refs = (v0_ref, v1_ref, v2_ref)
    nb = len(k_refs)
    pad = [jnp.where(b < (nb - 1) - i, NEG_INF, 0.0).astype(jnp.float32)
           for b in range(nb)]
    for h in range(heads):
        sl = slice(h * dh, (h + 1) * dh)
        q = q_ref[:, sl]
        s = []
        for b in range(nb):
            sb = _dot_nt(q, k_refs[b][:, sl]) + bias_ref[h, :, b * tq:(b + 1) * tq]
            if b < nb - 1:
                sb = sb + pad[b]
            s.append(sb)
        m = s[0].max(axis=-1, keepdims=True)
        for b in range(1, nb):
            m = jnp.maximum(m, s[b].max(axis=-1, keepdims=True))
        l = None
        o = None
        for b in range(nb):
            p = jnp.exp2(s[b] - m)
            pl_sum = p.sum(axis=-1, keepdims=True)
            po = _dot(p.astype(jnp.bfloat16), v_refs[b][:, sl])
            l = pl_sum if l is None else l + pl_sum
            o = po if o is None else o + po
        o_ref[:, sl] = (o / l).astype(o_ref.dtype)


def _band_attn(qkv, bias, d_model):
    s = qkv.shape[0]
    tq = A_TQ
    hg = A_HEAD_GROUP
    dh = d_model // A_HEADS
    gw = hg * dh
    n_groups = A_HEADS // hg
    ncol = d_model // gw
    nb = A_WIN_BLOCKS

    def kv_spec(section, b):
        return pl.BlockSpec(
            (tq, gw),
            lambda g, i: (jnp.maximum(i - (nb - 1) + b, 0), section * ncol + g))

    kernel = functools.partial(_band_attn_kernel, heads=hg, dh=dh, tq=tq)
    return pl.pallas_call(
        kernel,
        out_shape=jax.ShapeDtypeStruct((s, d_model), jnp.bfloat16),
        grid=(n_groups, s // tq),
        in_specs=[pl.BlockSpec((tq, gw), lambda g, i: (i, g))]
        + [kv_spec(1, b) for b in range(nb)]
        + [kv_spec(2, b) for b in range(nb)]
        + [pl.BlockSpec((hg, tq, nb * tq), lambda g, i: (g, 0, 0))],
        out_specs=pl.BlockSpec((tq, gw), lambda g, i: (i, g)),
        compiler_params=pltpu.CompilerParams(
            dimension_semantics=("parallel", "arbitrary"),
            vmem_limit_bytes=48 * MIB),
        name="band_attn",
    )(qkv, qkv, qkv, qkv, qkv, qkv, qkv, bias)


def _band_bias_table(rel_bias):
    tq, win = A_TQ, A_WIN_BLOCKS * A_TQ
    left = (A_WIN_BLOCKS - 1) * A_TQ
    rel = jnp.arange(tq + win - 1) - (tq - 1) - left
    idx = jnp.clip(rel, -A_REL_CLIP, A_REL_CLIP) + A_REL_CLIP
    profile = jnp.transpose(rel_bias[idx]).astype(jnp.float32) * LOG2E
    bias = _toeplitz(profile, tq, win)
    qc = jnp.arange(tq)[:, None] // CHUNK
    kc = (jnp.arange(win)[None, :] - left + A_LEFT_CHUNKS * CHUNK) // CHUNK
    allowed = (kc >= qc) & (kc <= qc + A_LEFT_CHUNKS)
    return jnp.where(allowed[None], bias, NEG_INF)


def _diff_attn_kernel(lam_ref, g_ref, q_ref, k_ref, v_ref, bd_ref, bl_ref, o_ref,
                      m_ref, l_ref, acc_ref, *, t, nsub, d, lambda_init):
    i = pl.program_id(1)
    e = 2 * d
    m_ref[...] = jnp.full(m_ref.shape, NEG_INF, jnp.float32)
    l_ref[...] = jnp.zeros(l_ref.shape, jnp.float32)
    acc_ref[...] = jnp.zeros(acc_ref.shape, jnp.float32)

    def tile_step(j, kinds):
        row = pl.multiple_of(j * t, t)
        kt = k_ref[pl.ds(row, t), :]
        vt = v_ref[pl.ds(row, t), :]
        for r, kind in enumerate(kinds):
            if kind is None:
                continue
            rows = slice(r * t, (r + 1) * t)
            for mp in range(2):
                s = _dot_nt(q_ref[rows, mp * d:(mp + 1) * d], kt[:, mp * d:(mp + 1) * d])
                if kind == "diag":
                    s = s + bd_ref[0]
                elif kind == "left":
                    s = s + bl_ref[0]
                m_old = m_ref[mp, rows]
                m_new = jnp.maximum(m_old, s.max(axis=-1, keepdims=True))
                alpha = jnp.exp2(m_old - m_new)
                p = jnp.exp2(s - _lane_tile(m_new, t // LANES))
                l_ref[mp, rows] = alpha * l_ref[mp, rows] + _lane_fold_sum(p)
                pv = _dot(p.astype(jnp.bfloat16), vt)
                acc_ref[mp, rows] = _lane_tile(alpha, e // LANES) * acc_ref[mp, rows] + pv
                m_ref[mp, rows] = m_new

    def body(j, carry):
        tile_step(j, ["plain"] * nsub)
        return carry

    first = nsub * i
    lax.fori_loop(0, jnp.maximum(first - 1, 0), body, 0)

    def kinds_at(u):
        out = []
        for r in range(nsub):
            rel = u - 1 - r
            out.append("plain" if rel < -1 else "left" if rel == -1
                       else "diag" if rel == 0 else None)
        return out

    @pl.when(i >= 1)
    def _():
        tile_step(first - 1, kinds_at(0))

    for u in range(1, nsub + 1):
        tile_step(first - 1 + u, kinds_at(u))

    lam = lam_ref[...]
    lam_full = (jnp.exp(jnp.sum(lam[0:1] * lam[1:2])) - jnp.exp(jnp.sum(lam[2:3] * lam[3:4]))
                + lambda_init)
    l1 = l_ref[0].sum(axis=-1, keepdims=True)
    l2 = l_ref[1].sum(axis=-1, keepdims=True)
    o = acc_ref[0] / l1 - lam_full * (acc_ref[1] / l2)
    o = _rmsnorm_f32(o, g_ref[...]) * (1.0 - lambda_init)
    o_ref[...] = o.astype(o_ref.dtype)


def _diff_attn(qkv, lam, subln_g, bias_diag, bias_left, d_model, lambda_init):
    s = qkv.shape[0]
    t = B_T
    nsub = B_NSUB
    tq = t * nsub
    e = d_model // B_HEADS
    d = e // 2
    kernel = functools.partial(_diff_attn_kernel, t=t, nsub=nsub, d=d,
                               lambda_init=lambda_init)
    return pl.pallas_call(
        kernel,
        out_shape=jax.ShapeDtypeStruct((s, d_model), jnp.bfloat16),
        grid=(B_HEADS, s // tq),
        in_specs=[
            pl.BlockSpec((4, d), lambda h, i: (0, 0)),
            pl.BlockSpec((1, e), lambda h, i: (0, 0)),
            pl.BlockSpec((tq, e), lambda h, i: (i, h)),
            pl.BlockSpec((s, e), lambda h, i: (0, B_HEADS + h)),
            pl.BlockSpec((s, e), lambda h, i: (0, 2 * B_HEADS + h)),
            pl.BlockSpec((1, t, t), lambda h, i: (h, 0, 0)),
            pl.BlockSpec((1, t, t), lambda h, i: (h, 0, 0)),
        ],
        out_specs=pl.BlockSpec((tq, e), lambda h, i: (i, h)),
        scratch_shapes=[
            pltpu.VMEM((2, tq, LANES), jnp.float32),
            pltpu.VMEM((2, tq, LANES), jnp.float32),
            pltpu.VMEM((2, tq, e), jnp.float32),
        ],
        compiler_params=pltpu.CompilerParams(
            dimension_semantics=("parallel", "arbitrary"),
            vmem_limit_bytes=56 * MIB),
        name="diff_attn",
    )(lam, subln_g, qkv, qkv, qkv, bias_diag, bias_left)


def _t5_bucket(rel):
    half = T5_BUCKETS // 2
    max_exact = half // 2
    ret = (rel > 0).astype(jnp.int32) * half
    n = jnp.abs(rel)
    nf = jnp.maximum(n, 1).astype(jnp.float32)
    large = max_exact + (jnp.log(nf / max_exact) / math.log(T5_MAX_DIST / max_exact)
                         * (half - max_exact)).astype(jnp.int32)
    large = jnp.minimum(large, half - 1)
    return ret + jnp.where(n < max_exact, n, large)


def _diff_bias_tables(t5_table):
    t = B_T
    far = t5_table[T5_BUCKETS // 2 - 1]

    def table(rel0):
        rel = jnp.arange(2 * t - 1) - (t - 1) + rel0
        profile = jnp.transpose(t5_table[_t5_bucket(rel)] - far).astype(jnp.float32) * LOG2E
        return _toeplitz(profile, t, t)

    pos = jnp.arange(t)
    allowed = (pos[None, :] // CHUNK) <= (pos[:, None] // CHUNK)
    diag = jnp.where(allowed[None], table(0), NEG_INF)
    left = table(-t)
    return diag, left


def _oproj_kernel(a_ref, w_ref, h_ref, g_ref, o_ref):
    y = _dot(a_ref[...], w_ref[...])
    o_ref[...] = h_ref[...] + _rmsnorm_f32(y, g_ref[...])


def _oproj(a, w, h, g):
    s, d = h.shape
    tm = min(OPROJ_TM, s)
    return pl.pallas_call(
        _oproj_kernel,
        out_shape=jax.ShapeDtypeStruct((s, d), jnp.float32),
        grid=(s // tm,),
        in_specs=[
            pl.BlockSpec((tm, d), lambda i: (i, 0)),
            pl.BlockSpec((d, d), lambda i: (0, 0)),
            pl.BlockSpec((tm, d), lambda i: (i, 0)),
            pl.BlockSpec((1, d), lambda i: (0, 0)),
        ],
        out_specs=pl.BlockSpec((tm, d), lambda i: (i, 0)),
        compiler_params=pltpu.CompilerParams(
            dimension_semantics=("parallel",),
            vmem_limit_bytes=48 * MIB),
        name="oproj",
    )(a, w, h, g)


def _mlp_kernel(h_ref, g_in_ref, wu_ref, wd_ref, g_out_ref, o_ref, xn_ref, acc_ref):
    c = pl.program_id(1)

    @pl.when(c == 0)
    def _():
        xn_ref[...] = _rmsnorm_f32(h_ref[...], g_in_ref[...]).astype(xn_ref.dtype)
        acc_ref[...] = jnp.zeros(acc_ref.shape, jnp.float32)

    u = jnp.maximum(_dot(xn_ref[...], wu_ref[...]), 0.0)
    acc_ref[...] += _dot((u * u).astype(jnp.bfloat16), wd_ref[...])

    @pl.when(c == pl.num_programs(1) - 1)
    def _():
        o_ref[...] = h_ref[...] + _rmsnorm_f32(acc_ref[...], g_out_ref[...])


def _mlp(h, g_in, wu, wd, g_out):
    s, d = h.shape
    f = wu.shape[1]
    tm, tf = min(MLP_TM, s), min(MLP_TF, f)
    return pl.pallas_call(
        _mlp_kernel,
        out_shape=jax.ShapeDtypeStruct((s, d), jnp.float32),
        grid=(s // tm, f // tf),
        in_specs=[
            pl.BlockSpec((tm, d), lambda i, c: (i, 0)),
            pl.BlockSpec((1, d), lambda i, c: (0, 0)),
            pl.BlockSpec((d, tf), lambda i, c: (0, c)),
            pl.BlockSpec((tf, d), lambda i, c: (c, 0)),
            pl.BlockSpec((1, d), lambda i, c: (0, 0)),
        ],
        out_specs=pl.BlockSpec((tm, d), lambda i, c: (i, 0)),
        scratch_shapes=[
            pltpu.VMEM((tm, d), jnp.bfloat16),
            pltpu.VMEM((tm, d), jnp.float32),
        ],
        compiler_params=pltpu.CompilerParams(
            dimension_semantics=("parallel", "arbitrary"),
            vmem_limit_bytes=48 * MIB),
        name="mlp",
    )(h, g_in, wu, wd, g_out)


def _scaled_qkv_weight(w_qkv, d_model, head_dim):
    scale = head_dim ** -0.5 * LOG2E
    col_scale = jnp.where(jnp.arange(w_qkv.shape[1]) < d_model, scale, 1.0)
    return (w_qkv * col_scale[None, :].astype(w_qkv.dtype)).astype(jnp.bfloat16)


def kernel(x, norm_g, a_w_qkv, a_w_o, a_rel_bias, b_w_qkv, b_w_o, b_lambda, b_subln_g,
           t5_bias, w_up, w_down):
    batch, seq, d_model = x.shape
    depth = norm_g.shape[0]
    n_mixers = 2
    assert seq % max(QKV_TM, OPROJ_TM, MLP_TM, A_TQ, B_T * B_NSUB) == 0
    assert A_TQ * (A_WIN_BLOCKS - 1) >= A_LEFT_CHUNKS * CHUNK and A_TQ % CHUNK == 0
    assert B_T % CHUNK == 0 and B_T >= T5_MAX_DIST

    bias_diag, bias_left = _diff_bias_tables(t5_bias)
    outs = []
    for b in range(batch):
        h = x[b]
        for layer in range(depth):
            g = norm_g[layer]
            i = layer // n_mixers
            if layer % n_mixers == 0:
                wqkv = _scaled_qkv_weight(a_w_qkv[i], d_model, d_model // A_HEADS)
                qkv = _norm_matmul(h, g[0:1], wqkv)
                a = _band_attn(qkv, _band_bias_table(a_rel_bias[i]), d_model)
                wo = a_w_o[i]
            else:
                wqkv = _scaled_qkv_weight(b_w_qkv[i], d_model, d_model // (2 * B_HEADS))
                qkv = _norm_matmul(h, g[0:1], wqkv)
                lambda_init = 0.8 - 0.6 * math.exp(-0.3 * layer)
                a = _diff_attn(qkv, b_lambda[i], b_subln_g[i][None, :], bias_diag, bias_left,
                               d_model, lambda_init)
                wo = b_w_o[i]
            h = _oproj(a, wo.astype(jnp.bfloat16), h, g[1:2])
            h = _mlp(h, g[2:3], w_up[layer].astype(jnp.bfloat16),
                     w_down[layer].astype(jnp.bfloat16), g[3:4])
        outs.append(h)
    return jnp.stack(outs, axis=0)
```

```python
import functools
import math

import jax
import jax.numpy as jnp
from jax import lax
from jax.experimental import pallas as pl
from jax.experimental.pallas import tpu as pltpu

EPS = 1e-6
NEG_INF = -1e30
LOG2E = math.log2(math.e)
LANES = 128

CHUNK = 64
A_HEADS = 16
A_LEFT_CHUNKS = 8
A_REL_CLIP = 128
B_HEADS = 8
T5_BUCKETS = 32
T5_MAX_DIST = 128

QKV_TM, QKV_TN = 1024, 1024
OPROJ_TM = 512
MLP_TM, MLP_TF = 1024, 512
A_TQ = 256
A_WIN_BLOCKS = 3
A_HEAD_GROUP = 8
B_T = 512
B_NSUB = 2

MIB = 1024 * 1024


def _dot(a, b):
    return jnp.dot(a, b, preferred_element_type=jnp.float32)


def _dot_nt(a, b):
    return lax.dot_general(a, b, (((1,), (1,)), ((), ())),
                           preferred_element_type=jnp.float32)


def _rmsnorm_f32(x, g):
    return x * lax.rsqrt(jnp.mean(x * x, axis=-1, keepdims=True) + EPS) * g


def _lane_tile(x, n):
    return x if n == 1 else jnp.concatenate([x] * n, axis=1)


def _lane_fold_sum(x):
    acc = x[:, :LANES]
    for c in range(1, x.shape[1] // LANES):
        acc = acc + x[:, c * LANES:(c + 1) * LANES]
    return acc


def _toeplitz(f, nrow, ncol):
    h, l = f.shape
    assert l == nrow + ncol - 1
    g = jnp.concatenate([f, jnp.zeros((h, 1), f.dtype)], axis=1)
    g = jnp.roll(g, -(nrow - 1), axis=1)
    flat = jnp.tile(g, (1, nrow))[:, :nrow * l]
    return flat.reshape(h, nrow, l)[:, :, :ncol]


def _norm_matmul_kernel(x_ref, g_ref, w_ref, o_ref, xn_ref):
    @pl.when(pl.program_id(1) == 0)
    def _():
        xn_ref[...] = _rmsnorm_f32(x_ref[...], g_ref[...]).astype(xn_ref.dtype)

    o_ref[...] = _dot(xn_ref[...], w_ref[...]).astype(o_ref.dtype)


def _norm_matmul(x, g, w):
    s, d = x.shape
    n = w.shape[1]
    tm, tn = min(QKV_TM, s), min(QKV_TN, n)
    return pl.pallas_call(
        _norm_matmul_kernel,
        out_shape=jax.ShapeDtypeStruct((s, n), jnp.bfloat16),
        grid=(s // tm, n // tn),
        in_specs=[
            pl.BlockSpec((tm, d), lambda i, j: (i, 0)),
            pl.BlockSpec((1, d), lambda i, j: (0, 0)),
            pl.BlockSpec((d, tn), lambda i, j: (0, j)),
        ],
        out_specs=pl.BlockSpec((tm, tn), lambda i, j: (i, j)),
        scratch_shapes=[pltpu.VMEM((tm, d), jnp.bfloat16)],
        compiler_params=pltpu.CompilerParams(
            dimension_semantics=("parallel", "arbitrary"),
            vmem_limit_bytes=48 * MIB),
        name="norm_qkv",
    )(x, g, w)


def _band_attn_kernel(q_ref, k0_ref, k1_ref, k2_ref, v0_ref, v1_ref, v2_ref,
                      bias_ref, o_ref, *, heads, dh, tq):
    i = pl.program_id(1)
    k_refs = (k0_ref, k1_ref, k2_ref)
    v_refs = (v0_ref, v1_ref, v2_ref)
    nb = len(k_refs)
    pad = [jnp.where(b < (nb - 1) - i, NEG_INF, 0.0).astype(jnp.float32)
           for b in range(nb)]
    def logits(h):
        sl = slice(h * dh, (h + 1) * dh)
        q = q_ref[:, sl]
        s = []
        for b in range(nb):
            sb = _dot_nt(q, k_refs[b][:, sl]) + bias_ref[h, :, b * tq:(b + 1) * tq]
            if b < nb - 1:
                sb = sb + pad[b]
            s.append(sb)
        return s

    s_next = logits(0)
    for h in range(heads):
        sl = slice(h * dh, (h + 1) * dh)
        s = s_next
        if h + 1 < heads:
            s_next = logits(h + 1)
        s_max = s[0]
        for b in range(1, nb):
            s_max = jnp.maximum(s_max, s[b])
        m = s_max.max(axis=-1, keepdims=True)
        l_part = None
        o = None
        for b in range(nb):
            p = jnp.exp2((s[b] - m).astype(jnp.bfloat16))
            p_part = _lane_fold_sum(p).astype(jnp.float32)
            po = _dot(p, v_refs[b][:, sl])
            l_part = p_part if l_part is None else l_part + p_part
            o = po if o is None else o + po
        l = l_part.sum(axis=-1, keepdims=True)
        o_ref[:, sl] = (o / l).astype(o_ref.dtype)


def _band_attn(qkv, bias, d_model):
    s = qkv.shape[0]
    tq = A_TQ
    hg = A_HEAD_GROUP
    dh = d_model // A_HEADS
    gw = hg * dh
    n_groups = A_HEADS // hg
    ncol = d_model // gw
    nb = A_WIN_BLOCKS

    def kv_spec(section, b):
        return pl.BlockSpec(
            (tq, gw),
            lambda g, i: (jnp.maximum(i - (nb - 1) + b, 0), section * ncol + g))

    kernel = functools.partial(_band_attn_kernel, heads=hg, dh=dh, tq=tq)
    return pl.pallas_call(
        kernel,
        out_shape=jax.ShapeDtypeStruct((s, d_model), jnp.bfloat16),
        grid=(n_groups, s // tq),
        in_specs=[pl.BlockSpec((tq, gw), lambda g, i: (i, g))]
        + [kv_spec(1, b) for b in range(nb)]
        + [kv_spec(2, b) for b in range(nb)]
        + [pl.BlockSpec((hg, tq, nb * tq), lambda g, i: (g, 0, 0))],
        out_specs=pl.BlockSpec((tq, gw), lambda g, i: (i, g)),
        compiler_params=pltpu.CompilerParams(
            dimension_semantics=("parallel", "arbitrary"),
            vmem_limit_bytes=48 * MIB),
        name="band_attn",
    )(qkv, qkv, qkv, qkv, qkv, qkv, qkv, bias)


def _band_bias_table(rel_bias):
    tq, win = A_TQ, A_WIN_BLOCKS * A_TQ
    left = (A_WIN_BLOCKS - 1) * A_TQ
    rel = jnp.arange(tq + win - 1) - (tq - 1) - left
    idx = jnp.clip(rel, -A_REL_CLIP, A_REL_CLIP) + A_REL_CLIP
    profile = jnp.transpose(rel_bias[idx]).astype(jnp.float32) * LOG2E
    bias = _toeplitz(profile, tq, win)
    qc = jnp.arange(tq)[:, None] // CHUNK
    kc = (jnp.arange(win)[None, :] - left + A_LEFT_CHUNKS * CHUNK) // CHUNK
    allowed = (kc >= qc) & (kc <= qc + A_LEFT_CHUNKS)
    return jnp.where(allowed[None], bias, NEG_INF)


def _diff_attn_kernel(lam_ref, g_ref, q_ref, k_ref, v_ref, bd_ref, bl_ref, o_ref,
                      m_ref, l_ref, acc_ref, pa_ref, aa_ref, pb_ref, ab_ref,
                      *, t, nsub, d, lambda_init):
    i = pl.program_id(1)
    e = 2 * d
    c = bl_ref.shape[-1]
    m_ref[...] = jnp.full(m_ref.shape, NEG_INF, jnp.float32)
    l_ref[...] = jnp.zeros(l_ref.shape, jnp.float32)
    acc_ref[...] = jnp.zeros(acc_ref.shape, jnp.float32)

    def key_rows(j):
        return pl.ds(pl.multiple_of(j * t, t), t)

    def logits(j, r, mp, kind):
        rows = slice(r * t, (r + 1) * t)
        s = _dot_nt(q_ref[rows, mp * d:(mp + 1) * d], k_ref[key_rows(j), mp * d:(mp + 1) * d])
        if kind == "diag":
            s = s + bd_ref[0]
        elif kind == "left":
            top = jnp.concatenate([s[:c, :t - c], s[:c, t - c:] + bl_ref[0]], axis=1)
            s = jnp.concatenate([top, s[c:]], axis=0)
        return s

    def softmax(r, mp, s):
        rows = slice(r * t, (r + 1) * t)
        m_old = m_ref[mp, rows]
        m_new = jnp.maximum(m_old, s.max(axis=-1, keepdims=True))
        m_ref[mp, rows] = m_new
        alpha = jnp.exp2(m_old - m_new)
        p = jnp.exp2((s - _lane_tile(m_new, t // LANES)).astype(jnp.bfloat16))
        l_ref[mp, rows] = alpha * l_ref[mp, rows] + _lane_fold_sum(p).astype(jnp.float32)
        return p, alpha

    def accumulate(j, r, mp, p, alpha):
        rows = slice(r * t, (r + 1) * t)
        pv = _dot(p, v_ref[key_rows(j), :])
        acc_ref[mp, rows] = _lane_tile(alpha, e // LANES) * acc_ref[mp, rows] + pv

    chains = [(r, mp) for r in range(nsub) for mp in range(2)]

    def stage(j_new, dst, j_old, src):
        s_next = logits(j_new, *chains[0], "plain") if dst is not None else None
        for n, (r, mp) in enumerate(chains):
            s = s_next
            if dst is not None and n + 1 < len(chains):
                s_next = logits(j_new, *chains[n + 1], "plain")
            if src is not None:
                accumulate(j_old, r, mp, src[0][n], src[1][n])
            if dst is not None:
                dst[0][n], dst[1][n] = softmax(r, mp, s)

    def run_chains(items, j_old=None, src=None):
        s_next = logits(*items[0])
        waiting = None
        for n, (j, r, mp, _) in enumerate(items):
            s = s_next
            if n + 1 < len(items):
                s_next = logits(*items[n + 1])
            if src is not None and n < len(chains):
                accumulate(j_old, *chains[n], src[0][n], src[1][n])
            if waiting is not None:
                accumulate(*waiting)
            waiting = (j, r, mp, *softmax(r, mp, s))
        accumulate(*waiting)

    def tile_items(j, kinds):
        return [(j, r, mp, kind) for r, kind in enumerate(kinds) if kind is not None
                for mp in range(2)]

    def kinds_at(u):
        out = []
        for r in range(nsub):
            rel = u - 1 - r
            out.append("plain" if rel < -1 else "left" if rel == -1
                       else "diag" if rel == 0 else None)
        return out

    first = nsub * i
    n_plain = first - 1

    buf_a, buf_b = (pa_ref, aa_ref), (pb_ref, ab_ref)

    @pl.when(i >= 1)
    def _():
        stage(0, buf_a, None, None)

    def body(jj, carry):
        j = 2 * jj
        stage(j + 1, buf_b, j, buf_a)
        stage(j + 2, buf_a, j + 1, buf_b)
        return carry

    lax.fori_loop(0, n_plain // 2, body, 0)

    @pl.when(i >= 1)
    def _():
        run_chains(tile_items(first - 1, kinds_at(0)), n_plain - 1, buf_a)

    tail = []
    for u in range(1, nsub + 1):
        tail += tile_items(first - 1 + u, kinds_at(u))
    run_chains(tail)

    lam = lam_ref[...]
    lam_full = (jnp.exp(jnp.sum(lam[0:1] * lam[1:2])) - jnp.exp(jnp.sum(lam[2:3] * lam[3:4]))
                + lambda_init)
    l1 = l_ref[0].sum(axis=-1, keepdims=True)
    l2 = l_ref[1].sum(axis=-1, keepdims=True)
    o = acc_ref[0] / l1 - lam_full * (acc_ref[1] / l2)
    o = _rmsnorm_f32(o, g_ref[...]) * (1.0 - lambda_init)
    o_ref[...] = o.astype(o_ref.dtype)


def _diff_attn(qkv, lam, subln_g, bias_diag, bias_left, d_model, lambda_init):
    s = qkv.shape[0]
    t = B_T
    nsub = B_NSUB
    tq = t * nsub
    e = d_model // B_HEADS
    d = e // 2
    kernel = functools.partial(_diff_attn_kernel, t=t, nsub=nsub, d=d,
                               lambda_init=lambda_init)
    return pl.pallas_call(
        kernel,
        out_shape=jax.ShapeDtypeStruct((s, d_model), jnp.bfloat16),
        grid=(B_HEADS, s // tq),
        in_specs=[
            pl.BlockSpec((4, d), lambda h, i: (0, 0)),
            pl.BlockSpec((1, e), lambda h, i: (0, 0)),
            pl.BlockSpec((tq, e), lambda h, i: (i, h)),
            pl.BlockSpec((s, e), lambda h, i: (0, B_HEADS + h)),
            pl.BlockSpec((s, e), lambda h, i: (0, 2 * B_HEADS + h)),
            pl.BlockSpec((1, t, t), lambda h, i: (h, 0, 0)),
            pl.BlockSpec((1,) + bias_left.shape[1:], lambda h, i: (h, 0, 0)),
        ],
        out_specs=pl.BlockSpec((tq, e), lambda h, i: (i, h)),
        scratch_shapes=[
            pltpu.VMEM((2, tq, LANES), jnp.float32),
            pltpu.VMEM((2, tq, LANES), jnp.float32),
            pltpu.VMEM((2, tq, e), jnp.float32),
            pltpu.VMEM((2 * nsub, t, t), jnp.bfloat16),
            pltpu.VMEM((2 * nsub, t, LANES), jnp.float32),
            pltpu.VMEM((2 * nsub, t, t), jnp.bfloat16),
            pltpu.VMEM((2 * nsub, t, LANES), jnp.float32),
        ],
        compiler_params=pltpu.CompilerParams(
            dimension_semantics=("parallel", "arbitrary"),
            vmem_limit_bytes=56 * MIB),
        name="diff_attn",
    )(lam, subln_g, qkv, qkv, qkv, bias_diag, bias_left)


def _t5_bucket(rel):
    half = T5_BUCKETS // 2
    max_exact = half // 2
    ret = (rel > 0).astype(jnp.int32) * half
    n = jnp.abs(rel)
    nf = jnp.maximum(n, 1).astype(jnp.float32)
    large = max_exact + (jnp.log(nf / max_exact) / math.log(T5_MAX_DIST / max_exact)
                         * (half - max_exact)).astype(jnp.int32)
    large = jnp.minimum(large, half - 1)
    return ret + jnp.where(n < max_exact, n, large)


def _diff_bias_tables(t5_table):
    t = B_T
    c = T5_MAX_DIST
    far = t5_table[T5_BUCKETS // 2 - 1]

    def table(rel0):
        rel = jnp.arange(2 * t - 1) - (t - 1) + rel0
        profile = jnp.transpose(t5_table[_t5_bucket(rel)] - far).astype(jnp.float32) * LOG2E
        return _toeplitz(profile, t, t)

    pos = jnp.arange(t)
    allowed = (pos[None, :] // CHUNK) <= (pos[:, None] // CHUNK)
    diag = jnp.where(allowed[None], table(0), NEG_INF)
    left = table(-t)[:, :c, t - c:]
    return diag, left


def _oproj_kernel(a_ref, w_ref, h_ref, g_ref, o_ref):
    y = _dot(a_ref[...], w_ref[...])
    o_ref[...] = h_ref[...] + _rmsnorm_f32(y, g_ref[...])


def _oproj(a, w, h, g):
    s, d = h.shape
    tm = min(OPROJ_TM, s)
    return pl.pallas_call(
        _oproj_kernel,
        out_shape=jax.ShapeDtypeStruct((s, d), jnp.float32),
        grid=(s // tm,),
        in_specs=[
            pl.BlockSpec((tm, d), lambda i: (i, 0)),
            pl.BlockSpec((d, d), lambda i: (0, 0)),
            pl.BlockSpec((tm, d), lambda i: (i, 0)),
            pl.BlockSpec((1, d), lambda i: (0, 0)),
        ],
        out_specs=pl.BlockSpec((tm, d), lambda i: (i, 0)),
        compiler_params=pltpu.CompilerParams(
            dimension_semantics=("parallel",),
            vmem_limit_bytes=48 * MIB),
        name="oproj",
    )(a, w, h, g)


def _mlp_kernel(h_ref, g_in_ref, wu_ref, wd_ref, g_out_ref, o_ref, xn_ref):
    c = pl.program_id(1)

    @pl.when(c == 0)
    def _():
        xn_ref[...] = _rmsnorm_f32(h_ref[...], g_in_ref[...]).astype(xn_ref.dtype)
        o_ref[...] = jnp.zeros(o_ref.shape, jnp.float32)

    u = jnp.maximum(_dot(xn_ref[...], wu_ref[...]), 0.0)
    o_ref[...] += _dot((u * u).astype(jnp.bfloat16), wd_ref[...])

    @pl.when(c == pl.num_programs(1) - 1)
    def _():
        o_ref[...] = h_ref[...] + _rmsnorm_f32(o_ref[...], g_out_ref[...])


def _mlp(h, g_in, wu, wd, g_out):
    s, d = h.shape
    f = wu.shape[1]
    tm, tf = min(MLP_TM, s), min(MLP_TF, f)
    return pl.pallas_call(
        _mlp_kernel,
        out_shape=jax.ShapeDtypeStruct((s, d), jnp.float32),
        grid=(s // tm, f // tf),
        in_specs=[
            pl.BlockSpec((tm, d), lambda i, c: (i, 0)),
            pl.BlockSpec((1, d), lambda i, c: (0, 0)),
            pl.BlockSpec((d, tf), lambda i, c: (0, c)),
            pl.BlockSpec((tf, d), lambda i, c: (c, 0)),
            pl.BlockSpec((1, d), lambda i, c: (0, 0)),
        ],
        out_specs=pl.BlockSpec((tm, d), lambda i, c: (i, 0)),
        scratch_shapes=[pltpu.VMEM((tm, d), jnp.bfloat16)],
        compiler_params=pltpu.CompilerParams(
            dimension_semantics=("parallel", "arbitrary"),
            vmem_limit_bytes=60 * MIB),
        name="mlp",
    )(h, g_in, wu, wd, g_out)


def _scaled_qkv_weight(w_qkv, d_model, head_dim):
    scale = head_dim ** -0.5 * LOG2E
    col_scale = jnp.where(jnp.arange(w_qkv.shape[1]) < d_model, scale, 1.0)
    return (w_qkv * col_scale[None, :].astype(w_qkv.dtype)).astype(jnp.bfloat16)


def kernel(x, norm_g, a_w_qkv, a_w_o, a_rel_bias, b_w_qkv, b_w_o, b_lambda, b_subln_g,
           t5_bias, w_up, w_down):
    batch, seq, d_model = x.shape
    depth = norm_g.shape[0]
    n_mixers = 2
    assert seq % max(QKV_TM, OPROJ_TM, MLP_TM, A_TQ, B_T * B_NSUB) == 0
    assert A_TQ * (A_WIN_BLOCKS - 1) >= A_LEFT_CHUNKS * CHUNK and A_TQ % CHUNK == 0
    assert B_T % CHUNK == 0 and B_T >= T5_MAX_DIST and B_NSUB % 2 == 0

    bias_diag, bias_left = _diff_bias_tables(t5_bias)
    outs = []
    for b in range(batch):
        h = x[b]
        for layer in range(depth):
            g = norm_g[layer]
            i = layer // n_mixers
            if layer % n_mixers == 0:
                wqkv = _scaled_qkv_weight(a_w_qkv[i], d_model, d_model // A_HEADS)
                qkv = _norm_matmul(h, g[0:1], wqkv)
                a = _band_attn(qkv, _band_bias_table(a_rel_bias[i]), d_model)
                wo = a_w_o[i]
            else:
                wqkv = _scaled_qkv_weight(b_w_qkv[i], d_model, d_model // (2 * B_HEADS))
                qkv = _norm_matmul(h, g[0:1], wqkv)
                lambda_init = 0.8 - 0.6 * math.exp(-0.3 * layer)
                a = _diff_attn(qkv, b_lambda[i], b_subln_g[i][None, :], bias_diag, bias_left,
                               d_model, lambda_init)
                wo = b_w_o[i]
            h = _oproj(a, wo.astype(jnp.bfloat16), h, g[1:2])
            h = _mlp(h, g[2:3], w_up[layer].astype(jnp.bfloat16),
                     w_down[layer].astype(jnp.bfloat16), g[3:4])
        outs.append(h)
    return jnp.stack(outs, axis=0)
```

```python
import functools
import math

import jax
import jax.numpy as jnp
from jax import lax
from jax.experimental import pallas as pl
from jax.experimental.pallas import tpu as pltpu

EPS = 1e-6
NEG_INF = -1e30
LOG2E = math.log2(math.e)
LANES = 128

CHUNK = 64
A_HEADS = 16
A_LEFT_CHUNKS = 8
A_REL_CLIP = 128
B_HEADS = 8
T5_BUCKETS = 32
T5_MAX_DIST = 128

QKV_TM, QKV_TN = 1024, 1024
OPROJ_TM = 512
MLP_TM, MLP_TF = 1024, 512
A_TQ = 256
A_WIN_BLOCKS = 3
A_HEAD_GROUP = 8
B_T = 512
B_NSUB = 2

NORM_IN_ROWS = 256
NORM_OUT_ROWS = 16

MIB = 1024 * 1024


def _dot(a, b):
    return jnp.dot(a, b, preferred_element_type=jnp.float32)


def _dot_nt(a, b):
    return lax.dot_general(a, b, (((1,), (1,)), ((), ())),
                           preferred_element_type=jnp.float32)


def _rmsnorm_f32(x, g):
    return x * lax.rsqrt(jnp.mean(x * x, axis=-1, keepdims=True) + EPS) * g


def _for_row_chunks(nrows, step, fn):
    step = min(step, nrows)
    for r0 in range(0, nrows, step):
        fn(slice(r0, r0 + step))


def _lane_tile(x, n):
    return x if n == 1 else jnp.concatenate([x] * n, axis=1)


def _lane_fold_sum(x):
    acc = x[:, :LANES]
    for c in range(1, x.shape[1] // LANES):
        acc = acc + x[:, c * LANES:(c + 1) * LANES]
    return acc


def _toeplitz(f, nrow, ncol):
    h, l = f.shape
    assert l == nrow + ncol - 1
    g = jnp.concatenate([f, jnp.zeros((h, 1), f.dtype)], axis=1)
    g = jnp.roll(g, -(nrow - 1), axis=1)
    flat = jnp.tile(g, (1, nrow))[:, :nrow * l]
    return flat.reshape(h, nrow, l)[:, :, :ncol]


def _norm_matmul_kernel(x_ref, g_ref, w_ref, cs_ref, o_ref, xn_ref):
    j = pl.program_id(1)
    half = x_ref.shape[0] // 2

    def project(rows):
        o_ref[rows, :] = (_dot(xn_ref[rows, :], w_ref[...]) * cs_ref[...]).astype(o_ref.dtype)

    @pl.when(j == 0)
    def _():
        for r in (slice(0, half), slice(half, 2 * half)):
            def norm(q, r=r):
                rows = slice(r.start + q.start, r.start + q.stop)
                xn_ref[rows, :] = _rmsnorm_f32(x_ref[rows, :], g_ref[...]).astype(xn_ref.dtype)

            _for_row_chunks(half, NORM_IN_ROWS, norm)
            project(r)

    @pl.when(j > 0)
    def _():
        project(slice(0, 2 * half))


def _norm_matmul(x, g, w, col_scale):
    s, d = x.shape
    n = w.shape[1]
    tm, tn = min(QKV_TM, s), min(QKV_TN, n)
    return pl.pallas_call(
        _norm_matmul_kernel,
        out_shape=jax.ShapeDtypeStruct((s, n), jnp.bfloat16),
        grid=(s // tm, n // tn),
        in_specs=[
            pl.BlockSpec((tm, d), lambda i, j: (i, 0)),
            pl.BlockSpec((1, d), lambda i, j: (0, 0)),
            pl.BlockSpec((d, tn), lambda i, j: (0, j)),
            pl.BlockSpec((1, tn), lambda i, j: (0, j)),
        ],
        out_specs=pl.BlockSpec((tm, tn), lambda i, j: (i, j)),
        scratch_shapes=[pltpu.VMEM((tm, d), jnp.bfloat16)],
        compiler_params=pltpu.CompilerParams(
            dimension_semantics=("parallel", "arbitrary"),
            vmem_limit_bytes=48 * MIB),
        name="norm_qkv",
    )(x, g, w, col_scale)


def _band_attn_kernel(q_ref, k0_ref, k1_ref, k2_ref, v0_ref, v1_ref, v2_ref,
                      bias_ref, o_ref, *, heads, dh, tq):
    i = pl.program_id(1)
    k_refs = (k0_ref, k1_ref, k2_ref)
    v_refs = (v0_ref, v1_ref, v2_ref)
    nb = len(k_refs)
    pad = [jnp.where(b < (nb - 1) - i, NEG_INF, 0.0).astype(jnp.float32)
           for b in range(nb)]
    def logits(h):
        sl = slice(h * dh, (h + 1) * dh)
        q = q_ref[:, sl]
        s = []
        for b in range(nb):
            sb = _dot_nt(q, k_refs[b][:, sl]) + bias_ref[h, :, b * tq:(b + 1) * tq]
            if b < nb - 1:
                sb = sb + pad[b]
            s.append(sb)
        return s

    s_next = logits(0)
    for h in range(heads):
        sl = slice(h * dh, (h + 1) * dh)
        s = s_next
        if h + 1 < heads:
            s_next = logits(h + 1)
        s_max = s[0]
        for b in range(1, nb):
            s_max = jnp.maximum(s_max, s[b])
        m = s_max.max(axis=-1, keepdims=True)
        l_part = None
        o = None
        for b in range(nb):
            p = jnp.exp2((s[b] - m).astype(jnp.bfloat16))
            p_part = _lane_fold_sum(p).astype(jnp.float32)
            po = _dot(p, v_refs[b][:, sl])
            l_part = p_part if l_part is None else l_part + p_part
            o = po if o is None else o + po
        l = l_part.sum(axis=-1, keepdims=True)
        o_ref[:, sl] = (o / l).astype(o_ref.dtype)


def _band_attn(qkv, bias, d_model):
    s = qkv.shape[0]
    tq = A_TQ
    hg = A_HEAD_GROUP
    dh = d_model // A_HEADS
    gw = hg * dh
    n_groups = A_HEADS // hg
    ncol = d_model // gw
    nb = A_WIN_BLOCKS

    def kv_spec(section, b):
        return pl.BlockSpec(
            (tq, gw),
            lambda g, i: (jnp.maximum(i - (nb - 1) + b, 0), section * ncol + g))

    kernel = functools.partial(_band_attn_kernel, heads=hg, dh=dh, tq=tq)
    return pl.pallas_call(
        kernel,
        out_shape=jax.ShapeDtypeStruct((s, d_model), jnp.bfloat16),
        grid=(n_groups, s // tq),
        in_specs=[pl.BlockSpec((tq, gw), lambda g, i: (i, g))]
        + [kv_spec(1, b) for b in range(nb)]
        + [kv_spec(2, b) for b in range(nb)]
        + [pl.BlockSpec((hg, tq, nb * tq), lambda g, i: (g, 0, 0))],
        out_specs=pl.BlockSpec((tq, gw), lambda g, i: (i, g)),
        compiler_params=pltpu.CompilerParams(
            dimension_semantics=("parallel", "arbitrary"),
            vmem_limit_bytes=48 * MIB),
        name="band_attn",
    )(qkv, qkv, qkv, qkv, qkv, qkv, qkv, bias)


def _band_bias_table(rel_bias):
    tq, win = A_TQ, A_WIN_BLOCKS * A_TQ
    left = (A_WIN_BLOCKS - 1) * A_TQ
    rel = jnp.arange(tq + win - 1) - (tq - 1) - left
    idx = jnp.clip(rel, -A_REL_CLIP, A_REL_CLIP) + A_REL_CLIP
    profile = jnp.transpose(rel_bias[idx]).astype(jnp.float32) * LOG2E
    bias = _toeplitz(profile, tq, win)
    qc = jnp.arange(tq)[:, None] // CHUNK
    kc = (jnp.arange(win)[None, :] - left + A_LEFT_CHUNKS * CHUNK) // CHUNK
    allowed = (kc >= qc) & (kc <= qc + A_LEFT_CHUNKS)
    return jnp.where(allowed[None], bias, NEG_INF)


def _diff_attn_kernel(lam_ref, g_ref, q_ref, k_ref, v_ref, bd_ref, bl_ref, o_ref,
                      m_ref, l_ref, acc_ref, pa_ref, aa_ref, pb_ref, ab_ref,
                      *, t, nsub, d, lambda_init):
    i = pl.program_id(1)
    e = 2 * d
    c = bl_ref.shape[-1]
    m_ref[...] = jnp.full(m_ref.shape, NEG_INF, jnp.float32)
    l_ref[...] = jnp.zeros(l_ref.shape, jnp.float32)
    acc_ref[...] = jnp.zeros(acc_ref.shape, jnp.float32)

    def key_rows(j):
        return pl.ds(pl.multiple_of(j * t, t), t)

    def logits(j, r, mp, kind):
        rows = slice(r * t, (r + 1) * t)
        s = _dot_nt(q_ref[rows, mp * d:(mp + 1) * d], k_ref[key_rows(j), mp * d:(mp + 1) * d])
        if kind == "diag":
            s = s + bd_ref[0]
        elif kind == "left":
            top = jnp.concatenate([s[:c, :t - c], s[:c, t - c:] + bl_ref[0]], axis=1)
            s = jnp.concatenate([top, s[c:]], axis=0)
        return s

    def softmax(r, mp, s):
        rows = slice(r * t, (r + 1) * t)
        m_old = m_ref[mp, rows]
        m_new = jnp.maximum(m_old, s.max(axis=-1, keepdims=True))
        m_ref[mp, rows] = m_new
        alpha = jnp.exp2(m_old - m_new)
        p = jnp.exp2((s - _lane_tile(m_new, t // LANES)).astype(jnp.bfloat16))
        l_ref[mp, rows] = alpha * l_ref[mp, rows] + _lane_fold_sum(p).astype(jnp.float32)
        return p, alpha

    def accumulate(j, r, mp, p, alpha):
        rows = slice(r * t, (r + 1) * t)
        pv = _dot(p, v_ref[key_rows(j), :])
        acc_ref[mp, rows] = _lane_tile(alpha, e // LANES) * acc_ref[mp, rows] + pv

    chains = [(r, mp) for r in range(nsub) for mp in range(2)]

    def stage(j_new, dst, j_old, src):
        s_next = logits(j_new, *chains[0], "plain") if dst is not None else None
        for n, (r, mp) in enumerate(chains):
            s = s_next
            if dst is not None and n + 1 < len(chains):
                s_next = logits(j_new, *chains[n + 1], "plain")
            if src is not None:
                accumulate(j_old, r, mp, src[0][n], src[1][n])
            if dst is not None:
                dst[0][n], dst[1][n] = softmax(r, mp, s)

    def run_chains(items, j_old=None, src=None):
        s_next = logits(*items[0])
        waiting = None
        for n, (j, r, mp, _) in enumerate(items):
            s = s_next
            if n + 1 < len(items):
                s_next = logits(*items[n + 1])
            if src is not None and n < len(chains):
                accumulate(j_old, *chains[n], src[0][n], src[1][n])
            if waiting is not None:
                accumulate(*waiting)
            waiting = (j, r, mp, *softmax(r, mp, s))
        accumulate(*waiting)

    def tile_items(j, kinds):
        return [(j, r, mp, kind) for r, kind in enumerate(kinds) if kind is not None
                for mp in range(2)]

    def kinds_at(u):
        out = []
        for r in range(nsub):
            rel = u - 1 - r
            out.append("plain" if rel < -1 else "left" if rel == -1
                       else "diag" if rel == 0 else None)
        return out

    first = nsub * i
    n_plain = first - 1

    buf_a, buf_b = (pa_ref, aa_ref), (pb_ref, ab_ref)

    @pl.when(i >= 1)
    def _():
        stage(0, buf_a, None, None)

    def body(jj, carry):
        j = 2 * jj
        stage(j + 1, buf_b, j, buf_a)
        stage(j + 2, buf_a, j + 1, buf_b)
        return carry

    lax.fori_loop(0, n_plain // 2, body, 0)

    @pl.when(i >= 1)
    def _():
        run_chains(tile_items(first - 1, kinds_at(0)), n_plain - 1, buf_a)

    tail = []
    for u in range(1, nsub + 1):
        tail += tile_items(first - 1 + u, kinds_at(u))
    run_chains(tail)

    lam = lam_ref[...]
    lam_full = (jnp.exp(jnp.sum(lam[0:1] * lam[1:2])) - jnp.exp(jnp.sum(lam[2:3] * lam[3:4]))
                + lambda_init)
    l1 = l_ref[0].sum(axis=-1, keepdims=True)
    l2 = l_ref[1].sum(axis=-1, keepdims=True)
    o = acc_ref[0] / l1 - lam_full * (acc_ref[1] / l2)
    o = _rmsnorm_f32(o, g_ref[...]) * (1.0 - lambda_init)
    o_ref[...] = o.astype(o_ref.dtype)


def _diff_attn(qkv, lam, subln_g, bias_diag, bias_left, d_model, lambda_init):
    s = qkv.shape[0]
    t = B_T
    nsub = B_NSUB
    tq = t * nsub
    e = d_model // B_HEADS
    d = e // 2
    kernel = functools.partial(_diff_attn_kernel, t=t, nsub=nsub, d=d,
                               lambda_init=lambda_init)
    return pl.pallas_call(
        kernel,
        out_shape=jax.ShapeDtypeStruct((s, d_model), jnp.bfloat16),
        grid=(B_HEADS, s // tq),
        in_specs=[
            pl.BlockSpec((4, d), lambda h, i: (0, 0)),
            pl.BlockSpec((1, e), lambda h, i: (0, 0)),
            pl.BlockSpec((tq, e), lambda h, i: (i, h)),
            pl.BlockSpec((s, e), lambda h, i: (0, B_HEADS + h)),
            pl.BlockSpec((s, e), lambda h, i: (0, 2 * B_HEADS + h)),
            pl.BlockSpec((1, t, t), lambda h, i: (h, 0, 0)),
            pl.BlockSpec((1,) + bias_left.shape[1:], lambda h, i: (h, 0, 0)),
        ],
        out_specs=pl.BlockSpec((tq, e), lambda h, i: (i, h)),
        scratch_shapes=[
            pltpu.VMEM((2, tq, LANES), jnp.float32),
            pltpu.VMEM((2, tq, LANES), jnp.float32),
            pltpu.VMEM((2, tq, e), jnp.float32),
            pltpu.VMEM((2 * nsub, t, t), jnp.bfloat16),
            pltpu.VMEM((2 * nsub, t, LANES), jnp.float32),
            pltpu.VMEM((2 * nsub, t, t), jnp.bfloat16),
            pltpu.VMEM((2 * nsub, t, LANES), jnp.float32),
        ],
        compiler_params=pltpu.CompilerParams(
            dimension_semantics=("parallel", "arbitrary"),
            vmem_limit_bytes=56 * MIB),
        name="diff_attn",
    )(lam, subln_g, qkv, qkv, qkv, bias_diag, bias_left)


def _t5_bucket(rel):
    half = T5_BUCKETS // 2
    max_exact = half // 2
    ret = (rel > 0).astype(jnp.int32) * half
    n = jnp.abs(rel)
    nf = jnp.maximum(n, 1).astype(jnp.float32)
    large = max_exact + (jnp.log(nf / max_exact) / math.log(T5_MAX_DIST / max_exact)
                         * (half - max_exact)).astype(jnp.int32)
    large = jnp.minimum(large, half - 1)
    return ret + jnp.where(n < max_exact, n, large)


def _diff_bias_tables(t5_table):
    t = B_T
    c = T5_MAX_DIST
    far = t5_table[T5_BUCKETS // 2 - 1]

    def table(rel0):
        rel = jnp.arange(2 * t - 1) - (t - 1) + rel0
        profile = jnp.transpose(t5_table[_t5_bucket(rel)] - far).astype(jnp.float32) * LOG2E
        return _toeplitz(profile, t, t)

    pos = jnp.arange(t)
    allowed = (pos[None, :] // CHUNK) <= (pos[:, None] // CHUNK)
    diag = jnp.where(allowed[None], table(0), NEG_INF)
    left = table(-t)[:, :c, t - c:]
    return diag, left


def _add_normed(o_ref, h_ref, g_ref, rows, y):
    def chunk(r):
        dst = slice(rows.start + r.start, rows.start + r.stop)
        o_ref[dst, :] = h_ref[dst, :] + _rmsnorm_f32(y[r, :], g_ref[...])

    _for_row_chunks(y.shape[0], NORM_OUT_ROWS, chunk)


def _oproj_kernel(a_ref, w_ref, h_ref, g_ref, o_ref):
    half = a_ref.shape[0] // 2
    halves = (slice(0, half), slice(half, 2 * half))
    ys = [_dot(a_ref[r, :], w_ref[...]) for r in halves]
    for r, y in zip(halves, ys):
        _add_normed(o_ref, h_ref, g_ref, r, y)


def _oproj(a, w, h, g):
    s, d = h.shape
    tm = min(OPROJ_TM, s)
    return pl.pallas_call(
        _oproj_kernel,
        out_shape=jax.ShapeDtypeStruct((s, d), jnp.float32),
        grid=(s // tm,),
        in_specs=[
            pl.BlockSpec((tm, d), lambda i: (i, 0)),
            pl.BlockSpec((d, d), lambda i: (0, 0)),
            pl.BlockSpec((tm, d), lambda i: (i, 0)),
            pl.BlockSpec((1, d), lambda i: (0, 0)),
        ],
        out_specs=pl.BlockSpec((tm, d), lambda i: (i, 0)),
        compiler_params=pltpu.CompilerParams(
            dimension_semantics=("parallel",),
            vmem_limit_bytes=48 * MIB),
        name="oproj",
    )(a, w, h, g)


def _mlp_kernel(h_ref, g_in_ref, wu_ref, wd_ref, g_out_ref, o_ref, xn_ref):
    c = pl.program_id(1)
    last = pl.num_programs(1) - 1
    half = h_ref.shape[0] // 2
    halves = (slice(0, half), slice(half, 2 * half))

    def mlp_part(xn):
        u = jnp.maximum(_dot(xn, wu_ref[...]), 0.0)
        return _dot((u * u).astype(jnp.bfloat16), wd_ref[...])

    @pl.when(c == 0)
    def _():
        for r in halves:
            def norm(q, r=r):
                rows = slice(r.start + q.start, r.start + q.stop)
                xn_ref[rows, :] = _rmsnorm_f32(h_ref[rows, :], g_in_ref[...]).astype(xn_ref.dtype)

            _for_row_chunks(half, NORM_IN_ROWS, norm)
            o_ref[r, :] = mlp_part(xn_ref[r, :])

    @pl.when(jnp.logical_and(c > 0, c < last))
    def _():
        o_ref[...] += mlp_part(xn_ref[...])

    @pl.when(c == last)
    def _():
        ys = [o_ref[r, :] + mlp_part(xn_ref[r, :]) for r in halves]
        for r, y in zip(halves, ys):
            _add_normed(o_ref, h_ref, g_out_ref, r, y)


def _mlp(h, g_in, wu, wd, g_out):
    s, d = h.shape
    f = wu.shape[1]
    tm, tf = min(MLP_TM, s), min(MLP_TF, f)
    return pl.pallas_call(
        _mlp_kernel,
        out_shape=jax.ShapeDtypeStruct((s, d), jnp.float32),
        grid=(s // tm, f // tf),
        in_specs=[
            pl.BlockSpec((tm, d), lambda i, c: (i, 0)),
            pl.BlockSpec((1, d), lambda i, c: (0, 0)),
            pl.BlockSpec((d, tf), lambda i, c: (0, c)),
            pl.BlockSpec((tf, d), lambda i, c: (c, 0)),
            pl.BlockSpec((1, d), lambda i, c: (0, 0)),
        ],
        out_specs=pl.BlockSpec((tm, d), lambda i, c: (i, 0)),
        scratch_shapes=[pltpu.VMEM((tm, d), jnp.bfloat16)],
        compiler_params=pltpu.CompilerParams(
            dimension_semantics=("parallel", "arbitrary"),
            vmem_limit_bytes=60 * MIB),
        name="mlp",
    )(h, g_in, wu, wd, g_out)


def _qkv_col_scale(n, d_model, head_dim):
    scale = head_dim ** -0.5 * LOG2E
    return jnp.where(jnp.arange(n) < d_model, scale, 1.0).astype(jnp.float32)[None, :]


def kernel(x, norm_g, a_w_qkv, a_w_o, a_rel_bias, b_w_qkv, b_w_o, b_lambda, b_subln_g,
           t5_bias, w_up, w_down):
    batch, seq, d_model = x.shape
    depth = norm_g.shape[0]
    n_mixers = 2
    assert seq % max(QKV_TM, OPROJ_TM, MLP_TM, A_TQ, B_T * B_NSUB) == 0
    assert A_TQ * (A_WIN_BLOCKS - 1) >= A_LEFT_CHUNKS * CHUNK and A_TQ % CHUNK == 0
    assert B_T % CHUNK == 0 and B_T >= T5_MAX_DIST and B_NSUB % 2 == 0

    bias_diag, bias_left = _diff_bias_tables(t5_bias)
    outs = []
    for b in range(batch):
        h = x[b]
        for layer in range(depth):
            g = norm_g[layer]
            i = layer // n_mixers
            if layer % n_mixers == 0:
                qkv = _norm_matmul(h, g[0:1], a_w_qkv[i].astype(jnp.bfloat16),
                                   _qkv_col_scale(3 * d_model, d_model, d_model // A_HEADS))
                a = _band_attn(qkv, _band_bias_table(a_rel_bias[i]), d_model)
                wo = a_w_o[i]
            else:
                qkv = _norm_matmul(h, g[0:1], b_w_qkv[i].astype(jnp.bfloat16),
                                   _qkv_col_scale(3 * d_model, d_model, d_model // (2 * B_HEADS)))
                lambda_init = 0.8 - 0.6 * math.exp(-0.3 * layer)
                a = _diff_attn(qkv, b_lambda[i], b_subln_g[i][None, :], bias_diag, bias_left,
                               d_model, lambda_init)
                wo = b_w_o[i]
            h = _oproj(a, wo.astype(jnp.bfloat16), h, g[1:2])
            h = _mlp(h, g[2:3], w_up[layer].astype(jnp.bfloat16),
                     w_down[layer].astype(jnp.bfloat16), g[3:4])
        outs.append(h)
    return jnp.stack(outs, axis=0)
```

```python
import functools
import math

import jax
import jax.numpy as jnp
from jax import lax
from jax.experimental import pallas as pl
from jax.experimental.pallas import tpu as pltpu

EPS = 1e-6
NEG_INF = -1e30
LOG2E = math.log2(math.e)
LANES = 128

CHUNK = 64
A_HEADS = 16
A_LEFT_CHUNKS = 8
A_REL_CLIP = 128
B_HEADS = 8
T5_BUCKETS = 32
T5_MAX_DIST = 128

QKV_TM, QKV_TN = 1024, 1024
OPROJ_TM = 512
MLP_TM, MLP_TF = 1024, 512
A_TQ = 256
A_WIN_BLOCKS = 3
A_HEAD_GROUP = 8
B_T = 512
B_NSUB = 2

NORM_IN_ROWS = 256
NORM_OUT_ROWS = 16

MIB = 1024 * 1024


def _dot(a, b):
    return jnp.dot(a, b, preferred_element_type=jnp.float32)


def _dot_nt(a, b):
    return lax.dot_general(a, b, (((1,), (1,)), ((), ())),
                           preferred_element_type=jnp.float32)


def _rmsnorm_f32(x, g):
    return x * lax.rsqrt(jnp.mean(x * x, axis=-1, keepdims=True) + EPS) * g


def _for_row_chunks(nrows, step, fn):
    step = min(step, nrows)
    for r0 in range(0, nrows, step):
        fn(slice(r0, r0 + step))


def _lane_tile(x, n):
    return x if n == 1 else jnp.concatenate([x] * n, axis=1)


def _lane_fold_sum(x):
    acc = x[:, :LANES]
    for c in range(1, x.shape[1] // LANES):
        acc = acc + x[:, c * LANES:(c + 1) * LANES]
    return acc


def _toeplitz(f, nrow, ncol):
    h, l = f.shape
    assert l == nrow + ncol - 1
    g = jnp.concatenate([f, jnp.zeros((h, 1), f.dtype)], axis=1)
    g = jnp.roll(g, -(nrow - 1), axis=1)
    flat = jnp.tile(g, (1, nrow))[:, :nrow * l]
    return flat.reshape(h, nrow, l)[:, :, :ncol]


def _norm_matmul_kernel(x_ref, g_ref, w_ref, cs_ref, o_ref, xn_ref):
    j = pl.program_id(1)
    half = x_ref.shape[0] // 2

    def project(rows):
        o_ref[rows, :] = (_dot(xn_ref[rows, :], w_ref[...]) * cs_ref[...]).astype(o_ref.dtype)

    @pl.when(j == 0)
    def _():
        for r in (slice(0, half), slice(half, 2 * half)):
            def norm(q, r=r):
                rows = slice(r.start + q.start, r.start + q.stop)
                xn_ref[rows, :] = _rmsnorm_f32(x_ref[rows, :], g_ref[...]).astype(xn_ref.dtype)

            _for_row_chunks(half, NORM_IN_ROWS, norm)
            project(r)

    @pl.when(j > 0)
    def _():
        project(slice(0, 2 * half))


def _norm_matmul(x, g, w, col_scale):
    s, d = x.shape
    n = w.shape[1]
    tm, tn = min(QKV_TM, s), min(QKV_TN, n)
    return pl.pallas_call(
        _norm_matmul_kernel,
        out_shape=jax.ShapeDtypeStruct((s, n), jnp.bfloat16),
        grid=(s // tm, n // tn),
        in_specs=[
            pl.BlockSpec((tm, d), lambda i, j: (i, 0)),
            pl.BlockSpec((1, d), lambda i, j: (0, 0)),
            pl.BlockSpec((d, tn), lambda i, j: (0, j)),
            pl.BlockSpec((1, tn), lambda i, j: (0, j)),
        ],
        out_specs=pl.BlockSpec((tm, tn), lambda i, j: (i, j)),
        scratch_shapes=[pltpu.VMEM((tm, d), jnp.bfloat16)],
        compiler_params=pltpu.CompilerParams(
            dimension_semantics=("parallel", "arbitrary"),
            vmem_limit_bytes=48 * MIB),
        name="norm_qkv",
    )(x, g, w, col_scale)


def _band_attn_kernel(q_ref, k0_ref, k1_ref, k2_ref, v0_ref, v1_ref, v2_ref,
                      bias_ref, o_ref, *, heads, dh, tq):
    i = pl.program_id(1)
    k_refs = (k0_ref, k1_ref, k2_ref)
    v_refs = (v0_ref, v1_ref, v2_ref)
    nb = len(k_refs)
    pad = [jnp.where(b < (nb - 1) - i, NEG_INF, 0.0).astype(jnp.float32)
           for b in range(nb)]
    def logits(h):
        sl = slice(h * dh, (h + 1) * dh)
        q = q_ref[:, sl]
        s = []
        for b in range(nb):
            sb = _dot_nt(q, k_refs[b][:, sl]) + bias_ref[h, :, b * tq:(b + 1) * tq]
            if b < nb - 1:
                sb = sb + pad[b]
            s.append(sb)
        return s

    s_next = logits(0)
    for h in range(heads):
        sl = slice(h * dh, (h + 1) * dh)
        s = s_next
        if h + 1 < heads:
            s_next = logits(h + 1)
        s_max = s[0]
        for b in range(1, nb):
            s_max = jnp.maximum(s_max, s[b])
        m = s_max.max(axis=-1, keepdims=True)
        l_part = None
        o = None
        for b in range(nb):
            p = jnp.exp2((s[b] - m).astype(jnp.bfloat16))
            p_part = _lane_fold_sum(p).astype(jnp.float32)
            po = _dot(p, v_refs[b][:, sl])
            l_part = p_part if l_part is None else l_part + p_part
            o = po if o is None else o + po
        l = l_part.sum(axis=-1, keepdims=True)
        o_ref[:, sl] = (o / l).astype(o_ref.dtype)


def _band_attn(qkv, bias, d_model):
    s = qkv.shape[0]
    tq = A_TQ
    hg = A_HEAD_GROUP
    dh = d_model // A_HEADS
    gw = hg * dh
    n_groups = A_HEADS // hg
    ncol = d_model // gw
    nb = A_WIN_BLOCKS

    def kv_spec(section, b):
        return pl.BlockSpec(
            (tq, gw),
            lambda g, i: (jnp.maximum(i - (nb - 1) + b, 0), section * ncol + g))

    kernel = functools.partial(_band_attn_kernel, heads=hg, dh=dh, tq=tq)
    return pl.pallas_call(
        kernel,
        out_shape=jax.ShapeDtypeStruct((s, d_model), jnp.bfloat16),
        grid=(n_groups, s // tq),
        in_specs=[pl.BlockSpec((tq, gw), lambda g, i: (i, g))]
        + [kv_spec(1, b) for b in range(nb)]
        + [kv_spec(2, b) for b in range(nb)]
        + [pl.BlockSpec((hg, tq, nb * tq), lambda g, i: (g, 0, 0))],
        out_specs=pl.BlockSpec((tq, gw), lambda g, i: (i, g)),
        compiler_params=pltpu.CompilerParams(
            dimension_semantics=("parallel", "arbitrary"),
            vmem_limit_bytes=48 * MIB),
        name="band_attn",
    )(qkv, qkv, qkv, qkv, qkv, qkv, qkv, bias)


def _band_bias_table(rel_bias):
    tq, win = A_TQ, A_WIN_BLOCKS * A_TQ
    left = (A_WIN_BLOCKS - 1) * A_TQ
    rel = jnp.arange(tq + win - 1) - (tq - 1) - left
    idx = jnp.clip(rel, -A_REL_CLIP, A_REL_CLIP) + A_REL_CLIP
    profile = jnp.transpose(rel_bias[idx]).astype(jnp.float32) * LOG2E
    bias = _toeplitz(profile, tq, win)
    qc = jnp.arange(tq)[:, None] // CHUNK
    kc = (jnp.arange(win)[None, :] - left + A_LEFT_CHUNKS * CHUNK) // CHUNK
    allowed = (kc >= qc) & (kc <= qc + A_LEFT_CHUNKS)
    return jnp.where(allowed[None], bias, NEG_INF)


def _diff_attn_kernel(lam_ref, g_ref, q_ref, k_ref, v_ref, bd_ref, bl_ref, o_ref,
                      m_ref, l_ref, acc_ref, pa_ref, aa_ref, pb_ref, ab_ref,
                      *, t, nsub, d, lambda_init):
    i = pl.program_id(1)
    e = 2 * d
    c = bl_ref.shape[-1]
    m_ref[...] = jnp.full(m_ref.shape, NEG_INF, jnp.float32)
    l_ref[...] = jnp.zeros(l_ref.shape, jnp.float32)
    acc_ref[...] = jnp.zeros(acc_ref.shape, jnp.float32)

    def key_rows(j):
        return pl.ds(pl.multiple_of(j * t, t), t)

    def logits(j, r, mp, kind, ghosted=False):
        rows = slice(r * t, (r + 1) * t)
        s = _dot_nt(q_ref[rows, mp * d:(mp + 1) * d], k_ref[key_rows(j), mp * d:(mp + 1) * d])
        if kind == "diag":
            s = s + bd_ref[0]
        elif kind == "left":
            top = jnp.concatenate([s[:c, :t - c], s[:c, t - c:] + bl_ref[0]], axis=1)
            s = jnp.concatenate([top, s[c:]], axis=0)
        if ghosted:
            s = s + ghost
        return s

    def softmax(r, mp, s):
        rows = slice(r * t, (r + 1) * t)
        m_old = m_ref[mp, rows]
        m_new = jnp.maximum(m_old, s.max(axis=-1, keepdims=True))
        m_ref[mp, rows] = m_new
        alpha = jnp.exp2(m_old - m_new)
        p = jnp.exp2((s - _lane_tile(m_new, t // LANES)).astype(jnp.bfloat16))
        l_ref[mp, rows] = alpha * l_ref[mp, rows] + _lane_fold_sum(p).astype(jnp.float32)
        return p, alpha

    def accumulate(j, r, mp, p, alpha):
        rows = slice(r * t, (r + 1) * t)
        pv = _dot(p, v_ref[key_rows(j), :])
        acc_ref[mp, rows] = _lane_tile(alpha, e // LANES) * acc_ref[mp, rows] + pv

    chains = [(r, mp) for r in range(nsub) for mp in range(2)]

    def stage(j_new, dst, j_old, src, ghosted=False):
        s_next = logits(j_new, *chains[0], "plain", ghosted)
        for n, (r, mp) in enumerate(chains):
            s = s_next
            if n + 1 < len(chains):
                s_next = logits(j_new, *chains[n + 1], "plain", ghosted)
            if src is not None:
                accumulate(j_old, r, mp, src[0][n], src[1][n])
            dst[0][n], dst[1][n] = softmax(r, mp, s)

    def run_chains(items, j_old=None, src=None):
        s_next = logits(*items[0])
        waiting = None
        for n, (j, r, mp, *_) in enumerate(items):
            s = s_next
            if n + 1 < len(items):
                s_next = logits(*items[n + 1])
            if src is not None and n < len(chains):
                accumulate(j_old, *chains[n], src[0][n], src[1][n])
            if waiting is not None:
                accumulate(*waiting)
            waiting = (j, r, mp, *softmax(r, mp, s))
        accumulate(*waiting)

    def tile_items(j, kinds, ghosted=False):
        return [(j, r, mp, kind, ghosted) for r, kind in enumerate(kinds) if kind is not None
                for mp in range(2)]

    def kinds_at(u):
        out = []
        for r in range(nsub):
            rel = u - 1 - r
            out.append("plain" if rel < -1 else "left" if rel == -1
                       else "diag" if rel == 0 else None)
        return out

    first = nsub * i
    n_plain = first - 1
    ghost = jnp.where(i == 0, NEG_INF, 0.0).astype(jnp.float32)

    buf_a, buf_b = (pa_ref, aa_ref), (pb_ref, ab_ref)
    stage(0, buf_a, None, None, ghosted=True)

    def body(jj, carry):
        j = 2 * jj
        stage(j + 1, buf_b, j, buf_a)
        stage(j + 2, buf_a, j + 1, buf_b)
        return carry

    lax.fori_loop(0, n_plain // 2, body, 0)

    tail = tile_items(jnp.maximum(first - 1, 0), kinds_at(0), ghosted=True)
    for u in range(1, nsub + 1):
        tail += tile_items(first - 1 + u, kinds_at(u))
    run_chains(tail, jnp.maximum(n_plain - 1, 0), buf_a)

    lam = lam_ref[...]
    lam_full = (jnp.exp(jnp.sum(lam[0:1] * lam[1:2])) - jnp.exp(jnp.sum(lam[2:3] * lam[3:4]))
                + lambda_init)
    l1 = l_ref[0].sum(axis=-1, keepdims=True)
    l2 = l_ref[1].sum(axis=-1, keepdims=True)
    o = acc_ref[0] / l1 - lam_full * (acc_ref[1] / l2)
    o = _rmsnorm_f32(o, g_ref[...]) * (1.0 - lambda_init)
    o_ref[...] = o.astype(o_ref.dtype)


def _diff_attn(qkv, lam, subln_g, bias_diag, bias_left, d_model, lambda_init):
    s = qkv.shape[0]
    t = B_T
    nsub = B_NSUB
    tq = t * nsub
    e = d_model // B_HEADS
    d = e // 2
    kernel = functools.partial(_diff_attn_kernel, t=t, nsub=nsub, d=d,
                               lambda_init=lambda_init)
    return pl.pallas_call(
        kernel,
        out_shape=jax.ShapeDtypeStruct((s, d_model), jnp.bfloat16),
        grid=(B_HEADS, s // tq),
        in_specs=[
            pl.BlockSpec((4, d), lambda h, i: (0, 0)),
            pl.BlockSpec((1, e), lambda h, i: (0, 0)),
            pl.BlockSpec((tq, e), lambda h, i: (i, h)),
            pl.BlockSpec((s, e), lambda h, i: (0, B_HEADS + h)),
            pl.BlockSpec((s, e), lambda h, i: (0, 2 * B_HEADS + h)),
            pl.BlockSpec((1, t, t), lambda h, i: (h, 0, 0)),
            pl.BlockSpec((1,) + bias_left.shape[1:], lambda h, i: (h, 0, 0)),
        ],
        out_specs=pl.BlockSpec((tq, e), lambda h, i: (i, h)),
        scratch_shapes=[
            pltpu.VMEM((2, tq, LANES), jnp.float32),
            pltpu.VMEM((2, tq, LANES), jnp.float32),
            pltpu.VMEM((2, tq, e), jnp.float32),
            pltpu.VMEM((2 * nsub, t, t), jnp.bfloat16),
            pltpu.VMEM((2 * nsub, t, LANES), jnp.float32),
            pltpu.VMEM((2 * nsub, t, t), jnp.bfloat16),
            pltpu.VMEM((2 * nsub, t, LANES), jnp.float32),
        ],
        compiler_params=pltpu.CompilerParams(
            dimension_semantics=("parallel", "arbitrary"),
            vmem_limit_bytes=56 * MIB),
        name="diff_attn",
    )(lam, subln_g, qkv, qkv, qkv, bias_diag, bias_left)


def _t5_bucket(rel):
    half = T5_BUCKETS // 2
    max_exact = half // 2
    ret = (rel > 0).astype(jnp.int32) * half
    n = jnp.abs(rel)
    nf = jnp.maximum(n, 1).astype(jnp.float32)
    large = max_exact + (jnp.log(nf / max_exact) / math.log(T5_MAX_DIST / max_exact)
                         * (half - max_exact)).astype(jnp.int32)
    large = jnp.minimum(large, half - 1)
    return ret + jnp.where(n < max_exact, n, large)


def _diff_bias_tables(t5_table):
    t = B_T
    c = T5_MAX_DIST
    far = t5_table[T5_BUCKETS // 2 - 1]

    def table(rel0):
        rel = jnp.arange(2 * t - 1) - (t - 1) + rel0
        profile = jnp.transpose(t5_table[_t5_bucket(rel)] - far).astype(jnp.float32) * LOG2E
        return _toeplitz(profile, t, t)

    pos = jnp.arange(t)
    allowed = (pos[None, :] // CHUNK) <= (pos[:, None] // CHUNK)
    diag = jnp.where(allowed[None], table(0), NEG_INF)
    left = table(-t)[:, :c, t - c:]
    return diag, left


def _add_normed(o_ref, h_ref, g_ref, rows, y):
    def chunk(r):
        dst = slice(rows.start + r.start, rows.start + r.stop)
        o_ref[dst, :] = h_ref[dst, :] + _rmsnorm_f32(y[r, :], g_ref[...])

    _for_row_chunks(y.shape[0], NORM_OUT_ROWS, chunk)


def _oproj_kernel(a_ref, w_ref, h_ref, g_ref, o_ref):
    y = _dot(a_ref[...], w_ref[...])
    _add_normed(o_ref, h_ref, g_ref, slice(0, y.shape[0]), y)


def _oproj(a, w, h, g):
    s, d = h.shape
    tm = min(OPROJ_TM, s)
    return pl.pallas_call(
        _oproj_kernel,
        out_shape=jax.ShapeDtypeStruct((s, d), jnp.float32),
        grid=(s // tm,),
        in_specs=[
            pl.BlockSpec((tm, d), lambda i: (i, 0)),
            pl.BlockSpec((d, d), lambda i: (0, 0)),
            pl.BlockSpec((tm, d), lambda i: (i, 0)),
            pl.BlockSpec((1, d), lambda i: (0, 0)),
        ],
        out_specs=pl.BlockSpec((tm, d), lambda i: (i, 0)),
        compiler_params=pltpu.CompilerParams(
            dimension_semantics=("parallel",),
            vmem_limit_bytes=48 * MIB),
        name="oproj",
    )(a, w, h, g)


def _mlp_kernel(h_ref, g_in_ref, wu_ref, wd_ref, g_out_ref, o_ref, xn_ref):
    c = pl.program_id(1)
    last = pl.num_programs(1) - 1
    half = h_ref.shape[0] // 2
    halves = (slice(0, half), slice(half, 2 * half))

    def mlp_part(xn):
        u = jnp.maximum(_dot(xn, wu_ref[...]), 0.0)
        return _dot((u * u).astype(jnp.bfloat16), wd_ref[...])

    @pl.when(c == 0)
    def _():
        for r in halves:
            def norm(q, r=r):
                rows = slice(r.start + q.start, r.start + q.stop)
                xn_ref[rows, :] = _rmsnorm_f32(h_ref[rows, :], g_in_ref[...]).astype(xn_ref.dtype)

            _for_row_chunks(half, NORM_IN_ROWS, norm)
            o_ref[r, :] = mlp_part(xn_ref[r, :])

    @pl.when(jnp.logical_and(c > 0, c < last))
    def _():
        o_ref[...] += mlp_part(xn_ref[...])

    @pl.when(c == last)
    def _():
        ys = [o_ref[r, :] + mlp_part(xn_ref[r, :]) for r in halves]
        for r, y in zip(halves, ys):
            _add_normed(o_ref, h_ref, g_out_ref, r, y)


def _mlp(h, g_in, wu, wd, g_out):
    s, d = h.shape
    f = wu.shape[1]
    tm, tf = min(MLP_TM, s), min(MLP_TF, f)
    return pl.pallas_call(
        _mlp_kernel,
        out_shape=jax.ShapeDtypeStruct((s, d), jnp.float32),
        grid=(s // tm, f // tf),
        in_specs=[
            pl.BlockSpec((tm, d), lambda i, c: (i, 0)),
            pl.BlockSpec((1, d), lambda i, c: (0, 0)),
            pl.BlockSpec((d, tf), lambda i, c: (0, c)),
            pl.BlockSpec((tf, d), lambda i, c: (c, 0)),
            pl.BlockSpec((1, d), lambda i, c: (0, 0)),
        ],
        out_specs=pl.BlockSpec((tm, d), lambda i, c: (i, 0)),
        scratch_shapes=[pltpu.VMEM((tm, d), jnp.bfloat16)],
        compiler_params=pltpu.CompilerParams(
            dimension_semantics=("parallel", "arbitrary"),
            vmem_limit_bytes=60 * MIB),
        name="mlp",
    )(h, g_in, wu, wd, g_out)


def _qkv_col_scale(n, d_model, head_dim):
    scale = head_dim ** -0.5 * LOG2E
    return jnp.where(jnp.arange(n) < d_model, scale, 1.0).astype(jnp.float32)[None, :]


def kernel(x, norm_g, a_w_qkv, a_w_o, a_rel_bias, b_w_qkv, b_w_o, b_lambda, b_subln_g,
           t5_bias, w_up, w_down):
    batch, seq, d_model = x.shape
    depth = norm_g.shape[0]
    n_mixers = 2
    assert seq % max(QKV_TM, OPROJ_TM, MLP_TM, A_TQ, B_T * B_NSUB) == 0
    assert A_TQ * (A_WIN_BLOCKS - 1) >= A_LEFT_CHUNKS * CHUNK and A_TQ % CHUNK == 0
    assert B_T % CHUNK == 0 and B_T >= T5_MAX_DIST and B_NSUB % 2 == 0

    bias_diag, bias_left = _diff_bias_tables(t5_bias)
    bf16 = jnp.bfloat16
    a_w_qkv, a_w_o, b_w_qkv, b_w_o, w_up, w_down = (
        w.astype(bf16) for w in (a_w_qkv, a_w_o, b_w_qkv, b_w_o, w_up, w_down))
    outs = []
    for b in range(batch):
        h = x[b]
        for layer in range(depth):
            g = norm_g[layer]
            i = layer // n_mixers
            if layer % n_mixers == 0:
                qkv = _norm_matmul(h, g[0:1], a_w_qkv[i],
                                   _qkv_col_scale(3 * d_model, d_model, d_model // A_HEADS))
                a = _band_attn(qkv, _band_bias_table(a_rel_bias[i]), d_model)
                wo = a_w_o[i]
            else:
                qkv = _norm_matmul(h, g[0:1], b_w_qkv[i],
                                   _qkv_col_scale(3 * d_model, d_model, d_model // (2 * B_HEADS)))
                lambda_init = 0.8 - 0.6 * math.exp(-0.3 * layer)
                a = _diff_attn(qkv, b_lambda[i], b_subln_g[i][None, :], bias_diag, bias_left,
                               d_model, lambda_init)
                wo = b_w_o[i]
            h = _oproj(a, wo, h, g[1:2])
            h = _mlp(h, g[2:3], w_up[layer], w_down[layer], g[3:4])
        outs.append(h)
    return jnp.stack(outs, axis=0)
```

```python
import functools
import math

import jax
import jax.numpy as jnp
from jax import lax
from jax.experimental import pallas as pl
from jax.experimental.pallas import tpu as pltpu

EPS = 1e-6
NEG_INF = -1e30
LOG2E = math.log2(math.e)
LANES = 128

CHUNK = 64
A_HEADS = 16
A_LEFT_CHUNKS = 8
A_REL_CLIP = 128
B_HEADS = 8
T5_BUCKETS = 32
T5_MAX_DIST = 128

QKV_TM, QKV_TN = 1024, 1024
OPROJ_TM = 512
MLP_TM, MLP_TF = 1024, 512
A_TQ = 256
A_WIN_BLOCKS = 3
A_HEAD_GROUP = 8
B_T = 512
B_NSUB = 2

NORM_IN_ROWS = 256
NORM_OUT_ROWS = 16

MIB = 1024 * 1024


def _dot(a, b):
    return jnp.dot(a, b, preferred_element_type=jnp.float32)


def _dot_nt(a, b):
    return lax.dot_general(a, b, (((1,), (1,)), ((), ())),
                           preferred_element_type=jnp.float32)


def _rmsnorm_f32(x, g):
    return x * lax.rsqrt(jnp.mean(x * x, axis=-1, keepdims=True) + EPS) * g


def _for_row_chunks(nrows, step, fn):
    step = min(step, nrows)
    for r0 in range(0, nrows, step):
        fn(slice(r0, r0 + step))


def _lane_tile(x, n):
    return x if n == 1 else jnp.concatenate([x] * n, axis=1)


def _lane_fold_sum(x):
    acc = x[:, :LANES]
    for c in range(1, x.shape[1] // LANES):
        acc = acc + x[:, c * LANES:(c + 1) * LANES]
    return acc


def _toeplitz_profile(f, nrow):
    h, l = f.shape
    p = 1 << (l - 1).bit_length()
    assert p > l - 1
    g = jnp.concatenate([f, jnp.zeros((h, p - l), f.dtype)], axis=1)
    return jnp.roll(g, -(nrow - 1), axis=1)[:, None, :]


def _toeplitz_rows(profile, nrow, ncol):
    x = jnp.broadcast_to(profile, (nrow, profile.shape[-1]))
    return pltpu.roll(x, 0, 1, stride=1, stride_axis=0)[:, :ncol]


def _norm_matmul_kernel(x_ref, g_ref, w_ref, cs_ref, o_ref, xn_ref):
    j = pl.program_id(1)
    half = x_ref.shape[0] // 2

    def project(rows):
        o_ref[rows, :] = (_dot(xn_ref[rows, :], w_ref[...]) * cs_ref[...]).astype(o_ref.dtype)

    @pl.when(j == 0)
    def _():
        for r in (slice(0, half), slice(half, 2 * half)):
            def norm(q, r=r):
                rows = slice(r.start + q.start, r.start + q.stop)
                xn_ref[rows, :] = _rmsnorm_f32(x_ref[rows, :], g_ref[...]).astype(xn_ref.dtype)

            _for_row_chunks(half, NORM_IN_ROWS, norm)
            project(r)

    @pl.when(j > 0)
    def _():
        project(slice(0, 2 * half))


def _norm_matmul(x, g, w, col_scale):
    s, d = x.shape
    n = w.shape[1]
    tm, tn = min(QKV_TM, s), min(QKV_TN, n)
    return pl.pallas_call(
        _norm_matmul_kernel,
        out_shape=jax.ShapeDtypeStruct((s, n), jnp.bfloat16),
        grid=(s // tm, n // tn),
        in_specs=[
            pl.BlockSpec((tm, d), lambda i, j: (i, 0)),
            pl.BlockSpec((1, d), lambda i, j: (0, 0)),
            pl.BlockSpec((d, tn), lambda i, j: (0, j)),
            pl.BlockSpec((1, tn), lambda i, j: (0, j)),
        ],
        out_specs=pl.BlockSpec((tm, tn), lambda i, j: (i, j)),
        scratch_shapes=[pltpu.VMEM((tm, d), jnp.bfloat16)],
        compiler_params=pltpu.CompilerParams(
            dimension_semantics=("parallel", "arbitrary"),
            vmem_limit_bytes=48 * MIB),
        name="norm_qkv",
    )(x, g, w, col_scale)


def _band_attn_kernel(q_ref, k0_ref, k1_ref, k2_ref, v0_ref, v1_ref, v2_ref,
                      prof_ref, mask_ref, o_ref, bias_ref, *, heads, dh, tq):
    i = pl.program_id(1)
    k_refs = (k0_ref, k1_ref, k2_ref)
    v_refs = (v0_ref, v1_ref, v2_ref)
    nb = len(k_refs)

    @pl.when(i == 0)
    def _():
        for h in range(heads):
            bias_ref[h] = _toeplitz_rows(prof_ref[h], tq, nb * tq) + mask_ref[...]

    pad = [jnp.where(b < (nb - 1) - i, NEG_INF, 0.0).astype(jnp.float32)
           for b in range(nb)]
    def logits(h):
        sl = slice(h * dh, (h + 1) * dh)
        q = q_ref[:, sl]
        s = []
        for b in range(nb):
            sb = _dot_nt(q, k_refs[b][:, sl]) + bias_ref[h, :, b * tq:(b + 1) * tq]
            if b < nb - 1:
                sb = sb + pad[b]
            s.append(sb)
        return s

    s_next = logits(0)
    for h in range(heads):
        sl = slice(h * dh, (h + 1) * dh)
        s = s_next
        if h + 1 < heads:
            s_next = logits(h + 1)
        s_max = s[0]
        for b in range(1, nb):
            s_max = jnp.maximum(s_max, s[b])
        m = s_max.max(axis=-1, keepdims=True)
        l_part = None
        o = None
        for b in range(nb):
            p = jnp.exp2((s[b] - m).astype(jnp.bfloat16))
            p_part = _lane_fold_sum(p).astype(jnp.float32)
            po = _dot(p, v_refs[b][:, sl])
            l_part = p_part if l_part is None else l_part + p_part
            o = po if o is None else o + po
        l = l_part.sum(axis=-1, keepdims=True)
        o_ref[:, sl] = (o / l).astype(o_ref.dtype)


def _band_attn(qkv, profile, mask, d_model):
    s = qkv.shape[0]
    tq = A_TQ
    hg = A_HEAD_GROUP
    dh = d_model // A_HEADS
    gw = hg * dh
    n_groups = A_HEADS // hg
    ncol = d_model // gw
    nb = A_WIN_BLOCKS

    def kv_spec(section, b):
        return pl.BlockSpec(
            (tq, gw),
            lambda g, i: (jnp.maximum(i - (nb - 1) + b, 0), section * ncol + g))

    kernel = functools.partial(_band_attn_kernel, heads=hg, dh=dh, tq=tq)
    return pl.pallas_call(
        kernel,
        out_shape=jax.ShapeDtypeStruct((s, d_model), jnp.bfloat16),
        grid=(n_groups, s // tq),
        in_specs=[pl.BlockSpec((tq, gw), lambda g, i: (i, g))]
        + [kv_spec(1, b) for b in range(nb)]
        + [kv_spec(2, b) for b in range(nb)]
        + [pl.BlockSpec((hg, 1, profile.shape[-1]), lambda g, i: (g, 0, 0)),
           pl.BlockSpec((tq, nb * tq), lambda g, i: (0, 0))],
        out_specs=pl.BlockSpec((tq, gw), lambda g, i: (i, g)),
        scratch_shapes=[pltpu.VMEM((hg, tq, nb * tq), jnp.float32)],
        compiler_params=pltpu.CompilerParams(
            dimension_semantics=("parallel", "arbitrary"),
            vmem_limit_bytes=48 * MIB),
        name="band_attn",
    )(qkv, qkv, qkv, qkv, qkv, qkv, qkv, profile, mask)


def _band_bias_parts(rel_bias):
    tq, win = A_TQ, A_WIN_BLOCKS * A_TQ
    left = (A_WIN_BLOCKS - 1) * A_TQ
    rel = jnp.arange(tq + win - 1) - (tq - 1) - left
    idx = jnp.clip(rel, -A_REL_CLIP, A_REL_CLIP) + A_REL_CLIP
    profile = jnp.transpose(rel_bias[idx]).astype(jnp.float32) * LOG2E
    qc = jnp.arange(tq)[:, None] // CHUNK
    kc = (jnp.arange(win)[None, :] - left + A_LEFT_CHUNKS * CHUNK) // CHUNK
    allowed = (kc >= qc) & (kc <= qc + A_LEFT_CHUNKS)
    return _toeplitz_profile(profile, tq), jnp.where(allowed, 0.0, NEG_INF).astype(jnp.float32)


def _diff_attn_kernel(lam_ref, g_ref, q_ref, k_ref, v_ref, pd_ref, pl_ref, mask_ref, o_ref,
                      m_ref, l_ref, acc_ref, pa_ref, aa_ref, pb_ref, ab_ref, bd_ref, bl_ref,
                      *, t, nsub, d, lambda_init):
    i = pl.program_id(1)
    e = 2 * d
    c = bl_ref.shape[-1]

    @pl.when(i == 0)
    def _():
        bd_ref[...] = _toeplitz_rows(pd_ref[0], t, t) + mask_ref[...]
        bl_ref[...] = _toeplitz_rows(pl_ref[0], c, c)
    m_ref[...] = jnp.full(m_ref.shape, NEG_INF, jnp.float32)
    l_ref[...] = jnp.zeros(l_ref.shape, jnp.float32)
    acc_ref[...] = jnp.zeros(acc_ref.shape, jnp.float32)

    def key_rows(j):
        return pl.ds(pl.multiple_of(j * t, t), t)

    def logits(j, r, mp, kind, ghosted=False):
        rows = slice(r * t, (r + 1) * t)
        s = _dot_nt(q_ref[rows, mp * d:(mp + 1) * d], k_ref[key_rows(j), mp * d:(mp + 1) * d])
        if kind == "diag":
            s = s + bd_ref[...]
        elif kind == "left":
            top = jnp.concatenate([s[:c, :t - c], s[:c, t - c:] + bl_ref[...]], axis=1)
            s = jnp.concatenate([top, s[c:]], axis=0)
        if ghosted:
            s = s + ghost
        return s

    def softmax(r, mp, s):
        rows = slice(r * t, (r + 1) * t)
        m_old = m_ref[mp, rows]
        m_new = jnp.maximum(m_old, s.max(axis=-1, keepdims=True))
        m_ref[mp, rows] = m_new
        alpha = jnp.exp2(m_old - m_new)
        p = jnp.exp2((s - _lane_tile(m_new, t // LANES)).astype(jnp.bfloat16))
        l_ref[mp, rows] = alpha * l_ref[mp, rows] + _lane_fold_sum(p).astype(jnp.float32)
        return p, alpha

    def accumulate(j, r, mp, p, alpha):
        rows = slice(r * t, (r + 1) * t)
        pv = _dot(p, v_ref[key_rows(j), :])
        acc_ref[mp, rows] = _lane_tile(alpha, e // LANES) * acc_ref[mp, rows] + pv

    chains = [(r, mp) for r in range(nsub) for mp in range(2)]

    def stage(j_new, dst, j_old, src, ghosted=False):
        s_next = logits(j_new, *chains[0], "plain", ghosted)
        for n, (r, mp) in enumerate(chains):
            s = s_next
            if n + 1 < len(chains):
                s_next = logits(j_new, *chains[n + 1], "plain", ghosted)
            if src is not None:
                accumulate(j_old, r, mp, src[0][n], src[1][n])
            dst[0][n], dst[1][n] = softmax(r, mp, s)

    def run_chains(items, j_old=None, src=None):
        s_next = logits(*items[0])
        waiting = None
        for n, (j, r, mp, *_) in enumerate(items):
            s = s_next
            if n + 1 < len(items):
                s_next = logits(*items[n + 1])
            if src is not None and n < len(chains):
                accumulate(j_old, *chains[n], src[0][n], src[1][n])
            if waiting is not None:
                accumulate(*waiting)
            waiting = (j, r, mp, *softmax(r, mp, s))
        accumulate(*waiting)

    def tile_items(j, kinds, ghosted=False):
        return [(j, r, mp, kind, ghosted) for r, kind in enumerate(kinds) if kind is not None
                for mp in range(2)]

    def kinds_at(u):
        out = []
        for r in range(nsub):
            rel = u - 1 - r
            out.append("plain" if rel < -1 else "left" if rel == -1
                       else "diag" if rel == 0 else None)
        return out

    first = nsub * i
    n_plain = first - 1
    ghost = jnp.where(i == 0, NEG_INF, 0.0).astype(jnp.float32)

    buf_a, buf_b = (pa_ref, aa_ref), (pb_ref, ab_ref)
    stage(0, buf_a, None, None, ghosted=True)

    def body(jj, carry):
        j = 2 * jj
        stage(j + 1, buf_b, j, buf_a)
        stage(j + 2, buf_a, j + 1, buf_b)
        return carry

    lax.fori_loop(0, n_plain // 2, body, 0)

    tail = tile_items(jnp.maximum(first - 1, 0), kinds_at(0), ghosted=True)
    for u in range(1, nsub + 1):
        tail += tile_items(first - 1 + u, kinds_at(u))
    run_chains(tail, jnp.maximum(n_plain - 1, 0), buf_a)

    lam = lam_ref[...]
    lam_full = (jnp.exp(jnp.sum(lam[0:1] * lam[1:2])) - jnp.exp(jnp.sum(lam[2:3] * lam[3:4]))
                + lambda_init)
    l1 = l_ref[0].sum(axis=-1, keepdims=True)
    l2 = l_ref[1].sum(axis=-1, keepdims=True)
    o = acc_ref[0] / l1 - lam_full * (acc_ref[1] / l2)
    o = _rmsnorm_f32(o, g_ref[...]) * (1.0 - lambda_init)
    o_ref[...] = o.astype(o_ref.dtype)


def _diff_attn(qkv, lam, subln_g, prof_diag, prof_left, mask, d_model, lambda_init):
    s = qkv.shape[0]
    t = B_T
    nsub = B_NSUB
    tq = t * nsub
    e = d_model // B_HEADS
    d = e // 2
    kernel = functools.partial(_diff_attn_kernel, t=t, nsub=nsub, d=d,
                               lambda_init=lambda_init)
    return pl.pallas_call(
        kernel,
        out_shape=jax.ShapeDtypeStruct((s, d_model), jnp.bfloat16),
        grid=(B_HEADS, s // tq),
        in_specs=[
            pl.BlockSpec((4, d), lambda h, i: (0, 0)),
            pl.BlockSpec((1, e), lambda h, i: (0, 0)),
            pl.BlockSpec((tq, e), lambda h, i: (i, h)),
            pl.BlockSpec((s, e), lambda h, i: (0, B_HEADS + h)),
            pl.BlockSpec((s, e), lambda h, i: (0, 2 * B_HEADS + h)),
            pl.BlockSpec((1, 1, prof_diag.shape[-1]), lambda h, i: (h, 0, 0)),
            pl.BlockSpec((1, 1, prof_left.shape[-1]), lambda h, i: (h, 0, 0)),
            pl.BlockSpec((t, t), lambda h, i: (0, 0)),
        ],
        out_specs=pl.BlockSpec((tq, e), lambda h, i: (i, h)),
        scratch_shapes=[
            pltpu.VMEM((2, tq, LANES), jnp.float32),
            pltpu.VMEM((2, tq, LANES), jnp.float32),
            pltpu.VMEM((2, tq, e), jnp.float32),
            pltpu.VMEM((2 * nsub, t, t), jnp.bfloat16),
            pltpu.VMEM((2 * nsub, t, LANES), jnp.float32),
            pltpu.VMEM((2 * nsub, t, t), jnp.bfloat16),
            pltpu.VMEM((2 * nsub, t, LANES), jnp.float32),
            pltpu.VMEM((t, t), jnp.float32),
            pltpu.VMEM((T5_MAX_DIST, T5_MAX_DIST), jnp.float32),
        ],
        compiler_params=pltpu.CompilerParams(
            dimension_semantics=("parallel", "arbitrary"),
            vmem_limit_bytes=56 * MIB),
        name="diff_attn",
    )(lam, subln_g, qkv, qkv, qkv, prof_diag, prof_left, mask)


def _t5_bucket(rel):
    half = T5_BUCKETS // 2
    max_exact = half // 2
    ret = (rel > 0).astype(jnp.int32) * half
    n = jnp.abs(rel)
    nf = jnp.maximum(n, 1).astype(jnp.float32)
    large = max_exact + (jnp.log(nf / max_exact) / math.log(T5_MAX_DIST / max_exact)
                         * (half - max_exact)).astype(jnp.int32)
    large = jnp.minimum(large, half - 1)
    return ret + jnp.where(n < max_exact, n, large)


def _diff_bias_parts(t5_table):
    t = B_T
    c = T5_MAX_DIST
    far = t5_table[T5_BUCKETS // 2 - 1]

    def profile(n, rel0):
        rel = jnp.arange(2 * n - 1) - (n - 1) + rel0
        f = jnp.transpose(t5_table[_t5_bucket(rel)] - far).astype(jnp.float32) * LOG2E
        return _toeplitz_profile(f, n)

    pos = jnp.arange(t)
    allowed = (pos[None, :] // CHUNK) <= (pos[:, None] // CHUNK)
    mask = jnp.where(allowed, 0.0, NEG_INF).astype(jnp.float32)
    return profile(t, 0), profile(c, -c), mask


def _add_normed(o_ref, h_ref, g_ref, rows, y):
    def chunk(r):
        dst = slice(rows.start + r.start, rows.start + r.stop)
        o_ref[dst, :] = h_ref[dst, :] + _rmsnorm_f32(y[r, :], g_ref[...])

    _for_row_chunks(y.shape[0], NORM_OUT_ROWS, chunk)


def _oproj_kernel(a_ref, w_ref, h_ref, g_ref, o_ref):
    y = _dot(a_ref[...], w_ref[...])
    _add_normed(o_ref, h_ref, g_ref, slice(0, y.shape[0]), y)


def _oproj(a, w, h, g):
    s, d = h.shape
    tm = min(OPROJ_TM, s)
    return pl.pallas_call(
        _oproj_kernel,
        out_shape=jax.ShapeDtypeStruct((s, d), jnp.float32),
        grid=(s // tm,),
        in_specs=[
            pl.BlockSpec((tm, d), lambda i: (i, 0)),
            pl.BlockSpec((d, d), lambda i: (0, 0)),
            pl.BlockSpec((tm, d), lambda i: (i, 0)),
            pl.BlockSpec((1, d), lambda i: (0, 0)),
        ],
        out_specs=pl.BlockSpec((tm, d), lambda i: (i, 0)),
        compiler_params=pltpu.CompilerParams(
            dimension_semantics=("parallel",),
            vmem_limit_bytes=48 * MIB),
        name="oproj",
    )(a, w, h, g)


def _mlp_kernel(h_ref, g_in_ref, wu_ref, wd_ref, g_out_ref, o_ref, xn_ref):
    c = pl.program_id(1)
    last = pl.num_programs(1) - 1
    half = h_ref.shape[0] // 2
    halves = (slice(0, half), slice(half, 2 * half))

    def mlp_part(xn):
        u = jnp.maximum(_dot(xn, wu_ref[...]), 0.0)
        return _dot((u * u).astype(jnp.bfloat16), wd_ref[...])

    @pl.when(c == 0)
    def _():
        for r in halves:
            def norm(q, r=r):
                rows = slice(r.start + q.start, r.start + q.stop)
                xn_ref[rows, :] = _rmsnorm_f32(h_ref[rows, :], g_in_ref[...]).astype(xn_ref.dtype)

            _for_row_chunks(half, NORM_IN_ROWS, norm)
            o_ref[r, :] = mlp_part(xn_ref[r, :])

    @pl.when(jnp.logical_and(c > 0, c < last))
    def _():
        o_ref[...] += mlp_part(xn_ref[...])

    @pl.when(c == last)
    def _():
        ys = [o_ref[r, :] + mlp_part(xn_ref[r, :]) for r in halves]
        for r, y in zip(halves, ys):
            _add_normed(o_ref, h_ref, g_out_ref, r, y)


def _mlp(h, g_in, wu, wd, g_out):
    s, d = h.shape
    f = wu.shape[1]
    tm, tf = min(MLP_TM, s), min(MLP_TF, f)
    return pl.pallas_call(
        _mlp_kernel,
        out_shape=jax.ShapeDtypeStruct((s, d), jnp.float32),
        grid=(s // tm, f // tf),
        in_specs=[
            pl.BlockSpec((tm, d), lambda i, c: (i, 0)),
            pl.BlockSpec((1, d), lambda i, c: (0, 0)),
            pl.BlockSpec((d, tf), lambda i, c: (0, c)),
            pl.BlockSpec((tf, d), lambda i, c: (c, 0)),
            pl.BlockSpec((1, d), lambda i, c: (0, 0)),
        ],
        out_specs=pl.BlockSpec((tm, d), lambda i, c: (i, 0)),
        scratch_shapes=[pltpu.VMEM((tm, d), jnp.bfloat16)],
        compiler_params=pltpu.CompilerParams(
            dimension_semantics=("parallel", "arbitrary"),
            vmem_limit_bytes=60 * MIB),
        name="mlp",
    )(h, g_in, wu, wd, g_out)


def _qkv_col_scale(n, d_model, head_dim):
    scale = head_dim ** -0.5 * LOG2E
    return jnp.where(jnp.arange(n) < d_model, scale, 1.0).astype(jnp.float32)[None, :]


def kernel(x, norm_g, a_w_qkv, a_w_o, a_rel_bias, b_w_qkv, b_w_o, b_lambda, b_subln_g,
           t5_bias, w_up, w_down):
    batch, seq, d_model = x.shape
    depth = norm_g.shape[0]
    n_mixers = 2
    assert seq % max(QKV_TM, OPROJ_TM, MLP_TM, A_TQ, B_T * B_NSUB) == 0
    assert A_TQ * (A_WIN_BLOCKS - 1) >= A_LEFT_CHUNKS * CHUNK and A_TQ % CHUNK == 0
    assert B_T % CHUNK == 0 and B_T >= T5_MAX_DIST and B_NSUB % 2 == 0

    prof_diag, prof_left, diag_mask = _diff_bias_parts(t5_bias)
    bf16 = jnp.bfloat16
    a_w_qkv, a_w_o, b_w_qkv, b_w_o, w_up, w_down = (
        w.astype(bf16) for w in (a_w_qkv, a_w_o, b_w_qkv, b_w_o, w_up, w_down))
    outs = []
    for b in range(batch):
        h = x[b]
        for layer in range(depth):
            g = norm_g[layer]
            i = layer // n_mixers
            if layer % n_mixers == 0:
                qkv = _norm_matmul(h, g[0:1], a_w_qkv[i],
                                   _qkv_col_scale(3 * d_model, d_model, d_model // A_HEADS))
                a = _band_attn(qkv, *_band_bias_parts(a_rel_bias[i]), d_model)
                wo = a_w_o[i]
            else:
                qkv = _norm_matmul(h, g[0:1], b_w_qkv[i],
                                   _qkv_col_scale(3 * d_model, d_model, d_model // (2 * B_HEADS)))
                lambda_init = 0.8 - 0.6 * math.exp(-0.3 * layer)
                a = _diff_attn(qkv, b_lambda[i], b_subln_g[i][None, :], prof_diag, prof_left,
                               diag_mask, d_model, lambda_init)
                wo = b_w_o[i]
            h = _oproj(a, wo, h, g[1:2])
            h = _mlp(h, g[2:3], w_up[layer], w_down[layer], g[3:4])
        outs.append(h)
    return jnp.stack(outs, axis=0)
```

```python
import functools
import math

import jax
import jax.numpy as jnp
from jax import lax
from jax.experimental import pallas as pl
from jax.experimental.pallas import tpu as pltpu

EPS = 1e-6
NEG_INF = -1e30
LOG2E = math.log2(math.e)
LANES = 128

CHUNK = 64
A_HEADS = 16
A_LEFT_CHUNKS = 8
A_REL_CLIP = 128
B_HEADS = 8
T5_BUCKETS = 32
T5_MAX_DIST = 128

QKV_TM, QKV_TN = 1024, 1024
OPROJ_TM = 512
MLP_TM, MLP_TF = 1024, 512
A_TQ = 256
A_WIN_BLOCKS = 3
A_HEAD_GROUP = 8
B_T = 512
B_NSUB = 2

NORM_IN_ROWS = 256
NORM_OUT_ROWS = 16

MIB = 1024 * 1024


def _dot(a, b):
    return jnp.dot(a, b, preferred_element_type=jnp.float32)


def _dot_nt(a, b):
    return lax.dot_general(a, b, (((1,), (1,)), ((), ())),
                           preferred_element_type=jnp.float32)


def _rmsnorm_f32(x, g):
    return x * lax.rsqrt(jnp.mean(x * x, axis=-1, keepdims=True) + EPS) * g


def _for_row_chunks(nrows, step, fn):
    step = min(step, nrows)
    for r0 in range(0, nrows, step):
        fn(slice(r0, r0 + step))


def _lane_tile(x, n):
    return x if n == 1 else jnp.concatenate([x] * n, axis=1)


def _lane_fold_sum(x):
    acc = x[:, :LANES]
    for c in range(1, x.shape[1] // LANES):
        acc = acc + x[:, c * LANES:(c + 1) * LANES]
    return acc


def _toeplitz_profile(f, nrow):
    h, l = f.shape
    p = 1 << (l - 1).bit_length()
    assert p > l - 1
    g = jnp.concatenate([f, jnp.zeros((h, p - l), f.dtype)], axis=1)
    return jnp.roll(g, -(nrow - 1), axis=1)[:, None, :]


def _toeplitz_rows(profile, nrow, ncol):
    x = jnp.broadcast_to(profile, (nrow, profile.shape[-1]))
    return pltpu.roll(x, 0, 1, stride=1, stride_axis=0)[:, :ncol]


def _norm_matmul_kernel(x_ref, g_ref, w_ref, cs_ref, o_ref, xn_ref):
    j = pl.program_id(1)
    half = x_ref.shape[0] // 2

    def project(rows):
        o_ref[rows, :] = (_dot(xn_ref[rows, :], w_ref[...]) * cs_ref[...]).astype(o_ref.dtype)

    @pl.when(j == 0)
    def _():
        for r in (slice(0, half), slice(half, 2 * half)):
            def norm(q, r=r):
                rows = slice(r.start + q.start, r.start + q.stop)
                xn_ref[rows, :] = _rmsnorm_f32(x_ref[rows, :], g_ref[...]).astype(xn_ref.dtype)

            _for_row_chunks(half, NORM_IN_ROWS, norm)
            project(r)

    @pl.when(j > 0)
    def _():
        project(slice(0, 2 * half))


def _norm_matmul(x, g, w, layer, col_scale):
    s, d = x.shape
    n = w.shape[2]
    tm, tn = min(QKV_TM, s), min(QKV_TN, n)
    return pl.pallas_call(
        _norm_matmul_kernel,
        out_shape=jax.ShapeDtypeStruct((s, n), jnp.bfloat16),
        grid=(s // tm, n // tn),
        in_specs=[
            pl.BlockSpec((tm, d), lambda i, j: (i, 0)),
            pl.BlockSpec((1, d), lambda i, j: (0, 0)),
            pl.BlockSpec((None, d, tn), lambda i, j: (layer, 0, j)),
            pl.BlockSpec((1, tn), lambda i, j: (0, j)),
        ],
        out_specs=pl.BlockSpec((tm, tn), lambda i, j: (i, j)),
        scratch_shapes=[pltpu.VMEM((tm, d), jnp.bfloat16)],
        compiler_params=pltpu.CompilerParams(
            dimension_semantics=("parallel", "arbitrary"),
            vmem_limit_bytes=48 * MIB),
        name="norm_qkv",
    )(x, g, w, col_scale)


def _band_attn_kernel(q_ref, k0_ref, k1_ref, k2_ref, v0_ref, v1_ref, v2_ref,
                      prof_ref, mask_ref, o_ref, bias_ref, *, heads, dh, tq):
    i = pl.program_id(1)
    k_refs = (k0_ref, k1_ref, k2_ref)
    v_refs = (v0_ref, v1_ref, v2_ref)
    nb = len(k_refs)

    @pl.when(i == 0)
    def _():
        for h in range(heads):
            bias_ref[h] = _toeplitz_rows(prof_ref[h], tq, nb * tq) + mask_ref[...]

    pad = [jnp.where(b < (nb - 1) - i, NEG_INF, 0.0).astype(jnp.float32)
           for b in range(nb)]
    def logits(h):
        sl = slice(h * dh, (h + 1) * dh)
        q = q_ref[:, sl]
        s = []
        for b in range(nb):
            sb = _dot_nt(q, k_refs[b][:, sl]) + bias_ref[h, :, b * tq:(b + 1) * tq]
            if b < nb - 1:
                sb = sb + pad[b]
            s.append(sb)
        return s

    s_next = logits(0)
    for h in range(heads):
        sl = slice(h * dh, (h + 1) * dh)
        s = s_next
        if h + 1 < heads:
            s_next = logits(h + 1)
        s_max = s[0]
        for b in range(1, nb):
            s_max = jnp.maximum(s_max, s[b])
        m = s_max.max(axis=-1, keepdims=True)
        l_part = None
        o = None
        for b in range(nb):
            p = jnp.exp2((s[b] - m).astype(jnp.bfloat16))
            p_part = _lane_fold_sum(p).astype(jnp.float32)
            po = _dot(p, v_refs[b][:, sl])
            l_part = p_part if l_part is None else l_part + p_part
            o = po if o is None else o + po
        l = l_part.sum(axis=-1, keepdims=True)
        o_ref[:, sl] = (o / l).astype(o_ref.dtype)


def _band_attn(qkv, profile, mask, d_model):
    s = qkv.shape[0]
    tq = A_TQ
    hg = A_HEAD_GROUP
    dh = d_model // A_HEADS
    gw = hg * dh
    n_groups = A_HEADS // hg
    ncol = d_model // gw
    nb = A_WIN_BLOCKS

    def kv_spec(section, b):
        return pl.BlockSpec(
            (tq, gw),
            lambda g, i: (jnp.maximum(i - (nb - 1) + b, 0), section * ncol + g))

    kernel = functools.partial(_band_attn_kernel, heads=hg, dh=dh, tq=tq)
    return pl.pallas_call(
        kernel,
        out_shape=jax.ShapeDtypeStruct((s, d_model), jnp.bfloat16),
        grid=(n_groups, s // tq),
        in_specs=[pl.BlockSpec((tq, gw), lambda g, i: (i, g))]
        + [kv_spec(1, b) for b in range(nb)]
        + [kv_spec(2, b) for b in range(nb)]
        + [pl.BlockSpec((hg, 1, profile.shape[-1]), lambda g, i: (g, 0, 0)),
           pl.BlockSpec((tq, nb * tq), lambda g, i: (0, 0))],
        out_specs=pl.BlockSpec((tq, gw), lambda g, i: (i, g)),
        scratch_shapes=[pltpu.VMEM((hg, tq, nb * tq), jnp.float32)],
        compiler_params=pltpu.CompilerParams(
            dimension_semantics=("parallel", "arbitrary"),
            vmem_limit_bytes=48 * MIB),
        name="band_attn",
    )(qkv, qkv, qkv, qkv, qkv, qkv, qkv, profile, mask)


def _band_bias_parts(rel_bias):
    tq, win = A_TQ, A_WIN_BLOCKS * A_TQ
    left = (A_WIN_BLOCKS - 1) * A_TQ
    rel = jnp.arange(tq + win - 1) - (tq - 1) - left
    idx = jnp.clip(rel, -A_REL_CLIP, A_REL_CLIP) + A_REL_CLIP
    profile = jnp.transpose(rel_bias[idx]).astype(jnp.float32) * LOG2E
    qc = jnp.arange(tq)[:, None] // CHUNK
    kc = (jnp.arange(win)[None, :] - left + A_LEFT_CHUNKS * CHUNK) // CHUNK
    allowed = (kc >= qc) & (kc <= qc + A_LEFT_CHUNKS)
    return _toeplitz_profile(profile, tq), jnp.where(allowed, 0.0, NEG_INF).astype(jnp.float32)


def _diff_attn_kernel(lam_ref, g_ref, q_ref, k_ref, v_ref, pd_ref, pl_ref, mask_ref, o_ref,
                      m_ref, l_ref, acc_ref, pa_ref, aa_ref, pb_ref, ab_ref, bd_ref, bl_ref,
                      *, t, nsub, d, lambda_init):
    i = pl.program_id(1)
    e = 2 * d
    c = bl_ref.shape[-1]

    @pl.when(i == 0)
    def _():
        bd_ref[...] = _toeplitz_rows(pd_ref[0], t, t) + mask_ref[...]
        bl_ref[...] = _toeplitz_rows(pl_ref[0], c, c)
    m_ref[...] = jnp.full(m_ref.shape, NEG_INF, jnp.float32)
    l_ref[...] = jnp.zeros(l_ref.shape, jnp.float32)
    acc_ref[...] = jnp.zeros(acc_ref.shape, jnp.float32)

    def key_rows(j):
        return pl.ds(pl.multiple_of(j * t, t), t)

    def logits(j, r, mp, kind, ghosted=False):
        rows = slice(r * t, (r + 1) * t)
        s = _dot_nt(q_ref[rows, mp * d:(mp + 1) * d], k_ref[key_rows(j), mp * d:(mp + 1) * d])
        if kind == "diag":
            s = s + bd_ref[...]
        elif kind == "left":
            top = jnp.concatenate([s[:c, :t - c], s[:c, t - c:] + bl_ref[...]], axis=1)
            s = jnp.concatenate([top, s[c:]], axis=0)
        if ghosted:
            s = s + ghost
        return s

    def softmax(r, mp, s):
        rows = slice(r * t, (r + 1) * t)
        m_old = m_ref[mp, rows]
        m_new = jnp.maximum(m_old, s.max(axis=-1, keepdims=True))
        m_ref[mp, rows] = m_new
        alpha = jnp.exp2(m_old - m_new)
        p = jnp.exp2((s - _lane_tile(m_new, t // LANES)).astype(jnp.bfloat16))
        l_ref[mp, rows] = alpha * l_ref[mp, rows] + _lane_fold_sum(p).astype(jnp.float32)
        return p, alpha

    def accumulate(j, r, mp, p, alpha):
        rows = slice(r * t, (r + 1) * t)
        pv = _dot(p, v_ref[key_rows(j), :])
        acc_ref[mp, rows] = _lane_tile(alpha, e // LANES) * acc_ref[mp, rows] + pv

    chains = [(r, mp) for r in range(nsub) for mp in range(2)]

    def stage(j_new, dst, j_old, src, ghosted=False):
        s_next = logits(j_new, *chains[0], "plain", ghosted)
        for n, (r, mp) in enumerate(chains):
            s = s_next
            if n + 1 < len(chains):
                s_next = logits(j_new, *chains[n + 1], "plain", ghosted)
            if src is not None:
                accumulate(j_old, r, mp, src[0][n], src[1][n])
            dst[0][n], dst[1][n] = softmax(r, mp, s)

    def run_chains(items, j_old=None, src=None):
        s_next = logits(*items[0])
        waiting = None
        for n, (j, r, mp, *_) in enumerate(items):
            s = s_next
            if n + 1 < len(items):
                s_next = logits(*items[n + 1])
            if src is not None and n < len(chains):
                accumulate(j_old, *chains[n], src[0][n], src[1][n])
            if waiting is not None:
                accumulate(*waiting)
            waiting = (j, r, mp, *softmax(r, mp, s))
        accumulate(*waiting)

    def tile_items(j, kinds, ghosted=False):
        return [(j, r, mp, kind, ghosted) for r, kind in enumerate(kinds) if kind is not None
                for mp in range(2)]

    def kinds_at(u):
        out = []
        for r in range(nsub):
            rel = u - 1 - r
            out.append("plain" if rel < -1 else "left" if rel == -1
                       else "diag" if rel == 0 else None)
        return out

    first = nsub * i
    n_plain = first - 1
    ghost = jnp.where(i == 0, NEG_INF, 0.0).astype(jnp.float32)

    buf_a, buf_b = (pa_ref, aa_ref), (pb_ref, ab_ref)
    stage(0, buf_a, None, None, ghosted=True)

    def body(jj, carry):
        j = 2 * jj
        stage(j + 1, buf_b, j, buf_a)
        stage(j + 2, buf_a, j + 1, buf_b)
        return carry

    lax.fori_loop(0, n_plain // 2, body, 0)

    tail = tile_items(jnp.maximum(first - 1, 0), kinds_at(0), ghosted=True)
    for u in range(1, nsub + 1):
        tail += tile_items(first - 1 + u, kinds_at(u))
    run_chains(tail, jnp.maximum(n_plain - 1, 0), buf_a)

    lam = lam_ref[...]
    lam_full = (jnp.exp(jnp.sum(lam[0:1] * lam[1:2])) - jnp.exp(jnp.sum(lam[2:3] * lam[3:4]))
                + lambda_init)
    l1 = l_ref[0].sum(axis=-1, keepdims=True)
    l2 = l_ref[1].sum(axis=-1, keepdims=True)
    o = acc_ref[0] / l1 - lam_full * (acc_ref[1] / l2)
    o = _rmsnorm_f32(o, g_ref[...]) * (1.0 - lambda_init)
    o_ref[...] = o.astype(o_ref.dtype)


def _diff_attn(qkv, lam, subln_g, prof_diag, prof_left, mask, d_model, lambda_init):
    s = qkv.shape[0]
    t = B_T
    nsub = B_NSUB
    tq = t * nsub
    e = d_model // B_HEADS
    d = e // 2
    kernel = functools.partial(_diff_attn_kernel, t=t, nsub=nsub, d=d,
                               lambda_init=lambda_init)
    return pl.pallas_call(
        kernel,
        out_shape=jax.ShapeDtypeStruct((s, d_model), jnp.bfloat16),
        grid=(B_HEADS, s // tq),
        in_specs=[
            pl.BlockSpec((4, d), lambda h, i: (0, 0)),
            pl.BlockSpec((1, e), lambda h, i: (0, 0)),
            pl.BlockSpec((tq, e), lambda h, i: (i, h)),
            pl.BlockSpec((s, e), lambda h, i: (0, B_HEADS + h)),
            pl.BlockSpec((s, e), lambda h, i: (0, 2 * B_HEADS + h)),
            pl.BlockSpec((1, 1, prof_diag.shape[-1]), lambda h, i: (h, 0, 0)),
            pl.BlockSpec((1, 1, prof_left.shape[-1]), lambda h, i: (h, 0, 0)),
            pl.BlockSpec((t, t), lambda h, i: (0, 0)),
        ],
        out_specs=pl.BlockSpec((tq, e), lambda h, i: (i, h)),
        scratch_shapes=[
            pltpu.VMEM((2, tq, LANES), jnp.float32),
            pltpu.VMEM((2, tq, LANES), jnp.float32),
            pltpu.VMEM((2, tq, e), jnp.float32),
            pltpu.VMEM((2 * nsub, t, t), jnp.bfloat16),
            pltpu.VMEM((2 * nsub, t, LANES), jnp.float32),
            pltpu.VMEM((2 * nsub, t, t), jnp.bfloat16),
            pltpu.VMEM((2 * nsub, t, LANES), jnp.float32),
            pltpu.VMEM((t, t), jnp.float32),
            pltpu.VMEM((T5_MAX_DIST, T5_MAX_DIST), jnp.float32),
        ],
        compiler_params=pltpu.CompilerParams(
            dimension_semantics=("parallel", "arbitrary"),
            vmem_limit_bytes=56 * MIB),
        name="diff_attn",
    )(lam, subln_g, qkv, qkv, qkv, prof_diag, prof_left, mask)


def _t5_bucket(rel):
    half = T5_BUCKETS // 2
    max_exact = half // 2
    ret = (rel > 0).astype(jnp.int32) * half
    n = jnp.abs(rel)
    nf = jnp.maximum(n, 1).astype(jnp.float32)
    large = max_exact + (jnp.log(nf / max_exact) / math.log(T5_MAX_DIST / max_exact)
                         * (half - max_exact)).astype(jnp.int32)
    large = jnp.minimum(large, half - 1)
    return ret + jnp.where(n < max_exact, n, large)


def _diff_bias_parts(t5_table):
    t = B_T
    c = T5_MAX_DIST
    far = t5_table[T5_BUCKETS // 2 - 1]

    def profile(n, rel0):
        rel = jnp.arange(2 * n - 1) - (n - 1) + rel0
        f = jnp.transpose(t5_table[_t5_bucket(rel)] - far).astype(jnp.float32) * LOG2E
        return _toeplitz_profile(f, n)

    pos = jnp.arange(t)
    allowed = (pos[None, :] // CHUNK) <= (pos[:, None] // CHUNK)
    mask = jnp.where(allowed, 0.0, NEG_INF).astype(jnp.float32)
    return profile(t, 0), profile(c, -c), mask


def _add_normed(o_ref, h_ref, g_ref, rows, y):
    def chunk(r):
        dst = slice(rows.start + r.start, rows.start + r.stop)
        o_ref[dst, :] = h_ref[dst, :] + _rmsnorm_f32(y[r, :], g_ref[...])

    _for_row_chunks(y.shape[0], NORM_OUT_ROWS, chunk)


def _oproj_kernel(a_ref, w_ref, h_ref, g_ref, o_ref):
    y = _dot(a_ref[...], w_ref[...])
    _add_normed(o_ref, h_ref, g_ref, slice(0, y.shape[0]), y)


def _oproj(a, w, layer, h, g):
    s, d = h.shape
    tm = min(OPROJ_TM, s)
    return pl.pallas_call(
        _oproj_kernel,
        out_shape=jax.ShapeDtypeStruct((s, d), jnp.float32),
        grid=(s // tm,),
        in_specs=[
            pl.BlockSpec((tm, d), lambda i: (i, 0)),
            pl.BlockSpec((None, d, d), lambda i: (layer, 0, 0)),
            pl.BlockSpec((tm, d), lambda i: (i, 0)),
            pl.BlockSpec((1, d), lambda i: (0, 0)),
        ],
        out_specs=pl.BlockSpec((tm, d), lambda i: (i, 0)),
        compiler_params=pltpu.CompilerParams(
            dimension_semantics=("parallel",),
            vmem_limit_bytes=48 * MIB),
        name="oproj",
    )(a, w, h, g)


def _mlp_kernel(h_ref, g_in_ref, wu_ref, wd_ref, g_out_ref, o_ref, xn_ref):
    c = pl.program_id(1)
    last = pl.num_programs(1) - 1
    half = h_ref.shape[0] // 2
    halves = (slice(0, half), slice(half, 2 * half))

    def mlp_part(xn):
        u = jnp.maximum(_dot(xn, wu_ref[...]), 0.0)
        return _dot((u * u).astype(jnp.bfloat16), wd_ref[...])

    @pl.when(c == 0)
    def _():
        for r in halves:
            def norm(q, r=r):
                rows = slice(r.start + q.start, r.start + q.stop)
                xn_ref[rows, :] = _rmsnorm_f32(h_ref[rows, :], g_in_ref[...]).astype(xn_ref.dtype)

            _for_row_chunks(half, NORM_IN_ROWS, norm)
            o_ref[r, :] = mlp_part(xn_ref[r, :])

    @pl.when(jnp.logical_and(c > 0, c < last))
    def _():
        o_ref[...] += mlp_part(xn_ref[...])

    @pl.when(c == last)
    def _():
        ys = [o_ref[r, :] + mlp_part(xn_ref[r, :]) for r in halves]
        for r, y in zip(halves, ys):
            _add_normed(o_ref, h_ref, g_out_ref, r, y)


def _mlp(h, g_in, wu, wd, layer, g_out):
    s, d = h.shape
    f = wu.shape[2]
    tm, tf = min(MLP_TM, s), min(MLP_TF, f)
    return pl.pallas_call(
        _mlp_kernel,
        out_shape=jax.ShapeDtypeStruct((s, d), jnp.float32),
        grid=(s // tm, f // tf),
        in_specs=[
            pl.BlockSpec((tm, d), lambda i, c: (i, 0)),
            pl.BlockSpec((1, d), lambda i, c: (0, 0)),
            pl.BlockSpec((None, d, tf), lambda i, c: (layer, 0, c)),
            pl.BlockSpec((None, tf, d), lambda i, c: (layer, c, 0)),
            pl.BlockSpec((1, d), lambda i, c: (0, 0)),
        ],
        out_specs=pl.BlockSpec((tm, d), lambda i, c: (i, 0)),
        scratch_shapes=[pltpu.VMEM((tm, d), jnp.bfloat16)],
        compiler_params=pltpu.CompilerParams(
            dimension_semantics=("parallel", "arbitrary"),
            vmem_limit_bytes=60 * MIB),
        name="mlp",
    )(h, g_in, wu, wd, g_out)


def _qkv_col_scale(n, d_model, head_dim):
    scale = head_dim ** -0.5 * LOG2E
    return jnp.where(jnp.arange(n) < d_model, scale, 1.0).astype(jnp.float32)[None, :]


def kernel(x, norm_g, a_w_qkv, a_w_o, a_rel_bias, b_w_qkv, b_w_o, b_lambda, b_subln_g,
           t5_bias, w_up, w_down):
    batch, seq, d_model = x.shape
    depth = norm_g.shape[0]
    n_mixers = 2
    assert seq % max(QKV_TM, OPROJ_TM, MLP_TM, A_TQ, B_T * B_NSUB) == 0
    assert A_TQ * (A_WIN_BLOCKS - 1) >= A_LEFT_CHUNKS * CHUNK and A_TQ % CHUNK == 0
    assert B_T % CHUNK == 0 and B_T >= T5_MAX_DIST and B_NSUB % 2 == 0

    prof_diag, prof_left, diag_mask = _diff_bias_parts(t5_bias)
    bf16 = jnp.bfloat16
    a_w_qkv, a_w_o, b_w_qkv, b_w_o, w_up, w_down = (
        w.astype(bf16) for w in (a_w_qkv, a_w_o, b_w_qkv, b_w_o, w_up, w_down))
    outs = []
    for b in range(batch):
        h = x[b]
        for layer in range(depth):
            g = norm_g[layer]
            i = layer // n_mixers
            if layer % n_mixers == 0:
                qkv = _norm_matmul(h, g[0:1], a_w_qkv, i,
                                   _qkv_col_scale(3 * d_model, d_model, d_model // A_HEADS))
                a = _band_attn(qkv, *_band_bias_parts(a_rel_bias[i]), d_model)
                wo = a_w_o
            else:
                qkv = _norm_matmul(h, g[0:1], b_w_qkv, i,
                                   _qkv_col_scale(3 * d_model, d_model, d_model // (2 * B_HEADS)))
                lambda_init = 0.8 - 0.6 * math.exp(-0.3 * layer)
                a = _diff_attn(qkv, b_lambda[i], b_subln_g[i][None, :], prof_diag, prof_left,
                               diag_mask, d_model, lambda_init)
                wo = b_w_o
            h = _oproj(a, wo, i, h, g[1:2])
            h = _mlp(h, g[2:3], w_up, w_down, layer, g[3:4])
        outs.append(h)
    return jnp.stack(outs, axis=0)
```
